```python
import math
import jax, jax.numpy as jnp
from jax import lax
import numpy as np

D_MODEL = 4096
BATCH = 1
SEQ = 16384
DEPTH = 4
DEC_BATCH = 16
DEC_SEQ = 16
PAST_LEN = 1024

CHUNK = 64
QBLOCK = 128
N_MIXERS = 3
RMS_EPS = 1e-6
ROPE_THETA = 10000.0

A_HEADS = 16
A_HEAD_DIM = 128
A_VDIM = 2 * A_HEAD_DIM
A_WIDTH = A_HEADS * A_VDIM
B_WIDTH = 2 * D_MODEL
B_HEAD_DIM = 64
B_HEADS = B_WIDTH // B_HEAD_DIM
B_GROUPS = 8
B_STATE = 128
B_CONV = 4
B_CONV_DIM = B_WIDTH + 2 * B_GROUPS * B_STATE
C_WIDTH = D_MODEL
C_GROUP = 16
C_GROUPS = C_WIDTH // C_GROUP
C_STATE = 64

F32 = jnp.float32

kernel_name = 'hybrid_diffattn_ssd_s5_stream_step'


def rmsnorm(x, w):
    xf = x.astype(F32)
    y = xf * lax.rsqrt(jnp.mean(xf * xf, axis=-1, keepdims=True) + RMS_EPS)
    return (y * w.astype(F32)).astype(x.dtype)


def rotary(x, pos):
    dh = x.shape[-1]
    inv = 1.0 / (ROPE_THETA ** (jnp.arange(0, dh, 2, dtype=F32) / dh))
    ang = pos.astype(F32)[:, None] * inv[None, :]
    shape = (1, pos.shape[0]) + (1,) * (x.ndim - 3) + (dh // 2,)
    cos = jnp.cos(ang).reshape(shape)
    sin = jnp.sin(ang).reshape(shape)
    xf = x.astype(F32)
    x1, x2 = xf[..., :dh // 2], xf[..., dh // 2:]
    return jnp.concatenate([x1 * cos - x2 * sin, x2 * cos + x1 * sin], axis=-1).astype(x.dtype)


def diff_attention_block(q, k, v, q_pos, k_pos, lam):
    s = jnp.einsum('bqhmd,bkhmd->bhmqk', q, k, preferred_element_type=F32) * (A_HEAD_DIM ** -0.5)
    visible = (k_pos[None, :] // CHUNK) <= (q_pos[:, None] // CHUNK)
    p = jax.nn.softmax(jnp.where(visible, s, -jnp.inf), axis=-1)
    a = p[:, :, 0] - lam * p[:, :, 1]
    return jnp.einsum('bhqk,bkhe->bqhe', a, v, preferred_element_type=F32)


def diff_attn_mixer(h, k_past, v_past, w_in, lq1, lk1, lq2, lk2, subln_w, w_out, lam_init):
    Bsz, L, _ = h.shape
    q, k, v, g = jnp.split(h @ w_in, 4, axis=-1)
    pos0 = 0 if k_past is None else k_past.shape[1]
    pos = pos0 + jnp.arange(L)
    q = rotary(q.reshape(Bsz, L, A_HEADS, 2, A_HEAD_DIM), pos)
    k = rotary(k.reshape(Bsz, L, A_HEADS, 2, A_HEAD_DIM), pos)
    v = v.reshape(Bsz, L, A_HEADS, A_VDIM)
    k_new = k.reshape(Bsz, L, A_HEADS, 2 * A_HEAD_DIM)
    if k_past is None:
        k_all, v_all, k_pos = k, v, pos
    else:
        P = k_past.shape[1]
        k_all = jnp.concatenate([k_past.reshape(Bsz, P, A_HEADS, 2, A_HEAD_DIM).astype(k.dtype), k], axis=1)
        v_all = jnp.concatenate([v_past.astype(v.dtype), v], axis=1)
        k_pos = jnp.arange(P + L)
    lam = (jnp.exp(jnp.sum(lq1.astype(F32) * lk1.astype(F32)))
           - jnp.exp(jnp.sum(lq2.astype(F32) * lk2.astype(F32))) + lam_init)
    qb = QBLOCK if L % QBLOCK == 0 else L
    nb = L // qb
    q_blocks = jnp.moveaxis(q.reshape(Bsz, nb, qb, A_HEADS, 2, A_HEAD_DIM), 1, 0)
    pos_blocks = pos.reshape(nb, qb)
    o = lax.map(lambda args: diff_attention_block(args[0], k_all, v_all, args[1], k_pos, lam),
                (q_blocks, pos_blocks))
    o = jnp.moveaxis(o, 0, 1).reshape(Bsz, L, A_HEADS, A_VDIM)
    o = rmsnorm(o, subln_w) * (1.0 - lam_init)
    y = (o.reshape(Bsz, L, A_WIDTH) * jax.nn.silu(g.astype(F32))).astype(h.dtype)
    return y @ w_out, k_new, v


def causal_depthwise_conv(x, prev, w, b):
    L = x.shape[1]
    xp = jnp.concatenate([prev.astype(x.dtype), x], axis=1)
    y = b
    for j in range(B_CONV):
        y = y + xp[:, j:j + L] * w[:, j]
    return y, xp[:, -(B_CONV - 1):]


def ssd_scan(x, dt, A, bm, cm, h0, T):
    Bsz, L, H, P = x.shape
    G, N = bm.shape[2], bm.shape[3]
    R = H // G
    nc = L // T
    causal = jnp.tril(jnp.ones((T, T), dtype=bool))

    def chunks(a):
        return jnp.moveaxis(a.reshape((Bsz, nc, T) + a.shape[2:]), 1, 0)

    def body(h, inp):
        xc, dtc, bc, cc = inp
        a = jnp.cumsum(dtc * A, axis=1)
        seg = a[:, :, None, :] - a[:, None, :, :]
        lm = jnp.exp(jnp.where(causal[None, :, :, None], seg, -jnp.inf)).reshape(Bsz, T, T, G, R)
        xdt = (xc * dtc[..., None]).reshape(Bsz, T, G, R, P)
        cb = jnp.einsum('btgn,bsgn->bgts', cc, bc)
        y_diag = jnp.einsum('bgts,btsgr,bsgrp->btgrp', cb, lm, xdt)
        hg = h.reshape(Bsz, G, R, P, N)
        y_off = jnp.einsum('btgn,bgrpn,btgr->btgrp', cc, hg, jnp.exp(a).reshape(Bsz, T, G, R))
        decay = jnp.exp(a[:, -1:, :] - a).reshape(Bsz, T, G, R)
        h_new = (hg * jnp.exp(a[:, -1]).reshape(Bsz, G, R, 1, 1)
                 + jnp.einsum('bsgn,bsgr,bsgrp->bgrpn', bc, decay, xdt))
        return h_new.reshape(Bsz, H, P, N), (y_diag + y_off).reshape(Bsz, T, H, P)

    h_last, y = lax.scan(body, h0, (chunks(x), chunks(dt), chunks(bm), chunks(cm)))
    return jnp.moveaxis(y, 0, 1).reshape(Bsz, L, H, P), h_last


def mamba2_mixer(h, conv_prev, ssm_prev, w_in, conv_w, conv_b, dt_bias, a_log, d_skip, norm_w, w_out):
    Bsz, L, _ = h.shape
    proj = h @ w_in
    z = proj[..., :B_WIDTH]
    xbc = proj[..., B_WIDTH:B_WIDTH + B_CONV_DIM]
    dt = proj[..., B_WIDTH + B_CONV_DIM:]
    if conv_prev is None:
        conv_prev = jnp.zeros((Bsz, B_CONV - 1, B_CONV_DIM), h.dtype)
    if ssm_prev is None:
        ssm_prev = jnp.zeros((Bsz, B_HEADS, B_HEAD_DIM, B_STATE), F32)
    xbc, conv_new = causal_depthwise_conv(xbc, conv_prev, conv_w, conv_b)
    xbc = jax.nn.silu(xbc)
    gn = B_GROUPS * B_STATE
    xs = xbc[..., :B_WIDTH].reshape(Bsz, L, B_HEADS, B_HEAD_DIM).astype(F32)
    bm = xbc[..., B_WIDTH:B_WIDTH + gn].reshape(Bsz, L, B_GROUPS, B_STATE).astype(F32)
    cm = xbc[..., B_WIDTH + gn:].reshape(Bsz, L, B_GROUPS, B_STATE).astype(F32)
    dt = jax.nn.softplus(dt.astype(F32) + dt_bias.astype(F32))
    A = -jnp.exp(a_log.astype(F32))
    T = CHUNK if L % CHUNK == 0 else L
    y, ssm_new = ssd_scan(xs, dt, A, bm, cm, ssm_prev.astype(F32), T)
    y = y + d_skip.astype(F32)[:, None] * xs
    y = y.reshape(Bsz, L, B_WIDTH) * jax.nn.silu(z.astype(F32))
    y = rmsnorm(y, norm_w).astype(h.dtype)
    return y @ w_out, conv_new, ssm_new


def cmul(ar, ai, br, bi):
    return ar * br - ai * bi, ar * bi + ai * br


def s5_combine(e1, e2):
    a1r, a1i, b1r, b1i = e1
    a2r, a2i, b2r, b2i = e2
    ar, ai = cmul(a2r, a2i, a1r, a1i)
    br, bi = cmul(a2r, a2i, b1r, b1i)
    return ar, ai, br + b2r, bi + b2i


def s5_mixer(h, s_re, s_im, w_in, lam_re, lam_im, log_step, b_re, b_im, c_re, c_im, d_skip, w_glu, b_glu, w_out):
    Bsz, L, _ = h.shape
    u, g = jnp.split(h @ w_in, 2, axis=-1)
    if s_re is None:
        s_re = jnp.zeros((Bsz, C_GROUPS, C_STATE), F32)
        s_im = jnp.zeros((Bsz, C_GROUPS, C_STATE), F32)
    lr, li = lam_re.astype(F32), lam_im.astype(F32)
    step = jnp.exp(log_step.astype(F32))[:, None]
    mag = jnp.exp(lr * step)
    abar_r, abar_i = mag * jnp.cos(li * step), mag * jnp.sin(li * step)
    den = lr * lr + li * li
    nr, ni = abar_r - 1.0, abar_i
    coef_r = (nr * lr + ni * li) / den
    coef_i = (ni * lr - nr * li) / den
    bbar_r, bbar_i = cmul(coef_r[..., None], coef_i[..., None], b_re.astype(F32), b_im.astype(F32))
    cr_w, ci_w = c_re.astype(F32), c_im.astype(F32)
    uf = u.astype(F32)
    T = CHUNK if L % CHUNK == 0 else L
    nc = L // T
    ub = jnp.moveaxis(uf.reshape(Bsz, nc, T, C_GROUPS, C_GROUP), 1, 0)

    def body(carry, ut):
        hr0, hi0 = carry
        bur = jnp.einsum('gpk,btgk->btgp', bbar_r, ut)
        bui = jnp.einsum('gpk,btgk->btgp', bbar_i, ut)
        ar = jnp.broadcast_to(abar_r, bur.shape)
        ai = jnp.broadcast_to(abar_i, bur.shape)
        pr, pi_, sr, si = lax.associative_scan(s5_combine, (ar, ai, bur, bui), axis=1)
        hr, hi = cmul(pr, pi_, hr0[:, None], hi0[:, None])
        hr, hi = hr + sr, hi + si
        y = jnp.einsum('gkp,btgp->btgk', cr_w, hr) - jnp.einsum('gkp,btgp->btgk', ci_w, hi)
        return (hr[:, -1], hi[:, -1]), y

    (hr_last, hi_last), y = lax.scan(body, (s_re.astype(F32), s_im.astype(F32)), ub)
    y = jnp.moveaxis(y, 0, 1).reshape(Bsz, L, C_WIDTH) + d_skip.astype(F32) * uf
    y = jax.nn.gelu(y)
    y = y * jax.nn.sigmoid(y @ w_glu.astype(F32) + b_glu.astype(F32))
    y = (y * jax.nn.silu(g.astype(F32))).astype(h.dtype)
    return y @ w_out, hr_last, hi_last


def setup_inputs(seed: int = 0) -> dict:
    key = jax.random.key(seed)
    counter = [0]

    def nk():
        counter[0] += 1
        return jax.random.fold_in(key, counter[0])

    def normal(shape, scale=1.0):
        return scale * jax.random.normal(nk(), shape, F32)

    def dense(fi, fo):
        return normal((fi, fo), fi ** -0.5)

    def gain(n):
        return 1.0 + normal((n,), 0.02)

    def log_uniform(shape, lo, hi):
        return jax.random.uniform(nk(), shape, F32, math.log(lo), math.log(hi))

    inp = {}
    inp['x_prompt'] = normal((BATCH, SEQ, D_MODEL))
    inp['x_sample'] = normal((DEC_BATCH, DEC_SEQ, D_MODEL))
    for i in range(DEPTH):
        kind = i % N_MIXERS
        if kind == 0:
            inp[f'cache_k_l{i}'] = normal((DEC_BATCH, PAST_LEN, A_HEADS, 2 * A_HEAD_DIM))
            inp[f'cache_v_l{i}'] = normal((DEC_BATCH, PAST_LEN, A_HEADS, A_VDIM))
        elif kind == 1:
            inp[f'state_conv_l{i}'] = normal((DEC_BATCH, B_CONV - 1, B_CONV_DIM))
            inp[f'state_ssm_l{i}'] = normal((DEC_BATCH, B_HEADS, B_HEAD_DIM, B_STATE), 0.1)
        else:
            inp[f'state_s5_re_l{i}'] = normal((DEC_BATCH, C_GROUPS, C_STATE), 0.1)
            inp[f'state_s5_im_l{i}'] = normal((DEC_BATCH, C_GROUPS, C_STATE), 0.1)
    for i in range(DEPTH):
        kind = i % N_MIXERS
        inp[f'norm_l{i}'] = gain(D_MODEL)
        if kind == 0:
            inp[f'attn_w_in_l{i}'] = dense(D_MODEL, 4 * A_WIDTH)
            for nm in ('lam_q1', 'lam_k1', 'lam_q2', 'lam_k2'):
                inp[f'{nm}_l{i}'] = normal((A_HEAD_DIM,), 0.1)
            inp[f'subln_l{i}'] = gain(A_VDIM)
            inp[f'attn_w_out_l{i}'] = dense(A_WIDTH, D_MODEL)
        elif kind == 1:
            inp[f'ssm_w_in_l{i}'] = dense(D_MODEL, B_WIDTH + B_CONV_DIM + B_HEADS)
            inp[f'conv_w_l{i}'] = normal((B_CONV_DIM, B_CONV), B_CONV ** -0.5)
            inp[f'conv_b_l{i}'] = normal((B_CONV_DIM,), 0.01)
            dt = jnp.exp(log_uniform((B_HEADS,), 1e-3, 1e-1))
            inp[f'dt_bias_l{i}'] = dt + jnp.log(-jnp.expm1(-dt))
            inp[f'a_log_l{i}'] = jnp.log(jax.random.uniform(nk(), (B_HEADS,), F32, 1.0, 16.0))
            inp[f'd_l{i}'] = gain(B_HEADS)
            inp[f'gnorm_l{i}'] = gain(B_WIDTH)
            inp[f'ssm_w_out_l{i}'] = dense(B_WIDTH, D_MODEL)
        else:
            inp[f's5_w_in_l{i}'] = dense(D_MODEL, 2 * C_WIDTH)
            inp[f'lam_re_l{i}'] = -0.5 + normal((C_GROUPS, C_STATE), 0.01)
            inp[f'lam_im_l{i}'] = math.pi * jnp.arange(C_STATE, dtype=F32)[None, :] + normal((C_GROUPS, C_STATE), 0.01)
            inp[f'log_step_l{i}'] = log_uniform((C_GROUPS,), 1e-3, 1e-1)
            inp[f'b_re_l{i}'] = normal((C_GROUPS, C_STATE, C_GROUP), (2 * C_GROUP) ** -0.5)
            inp[f'b_im_l{i}'] = normal((C_GROUPS, C_STATE, C_GROUP), (2 * C_GROUP) ** -0.5)
            inp[f'c_re_l{i}'] = normal((C_GROUPS, C_GROUP, C_STATE), C_STATE ** -0.5)
            inp[f'c_im_l{i}'] = normal((C_GROUPS, C_GROUP, C_STATE), C_STATE ** -0.5)
            inp[f'd_l{i}'] = normal((C_WIDTH,))
            inp[f'glu_w_l{i}'] = dense(C_WIDTH, C_WIDTH)
            inp[f'glu_b_l{i}'] = normal((C_WIDTH,), 0.01)
            inp[f's5_w_out_l{i}'] = dense(C_WIDTH, D_MODEL)
    inp['final_norm'] = gain(D_MODEL)
    return inp


def reference(x_prompt, x_sample,
              cache_k_l0, cache_v_l0, state_conv_l1, state_ssm_l1,
              state_s5_re_l2, state_s5_im_l2, cache_k_l3, cache_v_l3,
              norm_l0, attn_w_in_l0, lam_q1_l0, lam_k1_l0, lam_q2_l0, lam_k2_l0, subln_l0, attn_w_out_l0,
              norm_l1, ssm_w_in_l1, conv_w_l1, conv_b_l1, dt_bias_l1, a_log_l1, d_l1, gnorm_l1, ssm_w_out_l1,
              norm_l2, s5_w_in_l2, lam_re_l2, lam_im_l2, log_step_l2, b_re_l2, b_im_l2, c_re_l2, c_im_l2,
              d_l2, glu_w_l2, glu_b_l2, s5_w_out_l2,
              norm_l3, attn_w_in_l3, lam_q1_l3, lam_k1_l3, lam_q2_l3, lam_k2_l3, subln_l3, attn_w_out_l3,
              final_norm):
    layer_state = [(cache_k_l0, cache_v_l0), (state_conv_l1, state_ssm_l1),
                   (state_s5_re_l2, state_s5_im_l2), (cache_k_l3, cache_v_l3)]
    layer_weights = [
        (norm_l0, attn_w_in_l0, lam_q1_l0, lam_k1_l0, lam_q2_l0, lam_k2_l0, subln_l0, attn_w_out_l0),
        (norm_l1, ssm_w_in_l1, conv_w_l1, conv_b_l1, dt_bias_l1, a_log_l1, d_l1, gnorm_l1, ssm_w_out_l1),
        (norm_l2, s5_w_in_l2, lam_re_l2, lam_im_l2, log_step_l2, b_re_l2, b_im_l2, c_re_l2, c_im_l2,
         d_l2, glu_w_l2, glu_b_l2, s5_w_out_l2),
        (norm_l3, attn_w_in_l3, lam_q1_l3, lam_k1_l3, lam_q2_l3, lam_k2_l3, subln_l3, attn_w_out_l3),
    ]
    xp, xs = x_prompt, x_sample
    new_states = []
    for i in range(DEPTH):
        kind = i % N_MIXERS
        nw = layer_weights[i][0]
        mw = layer_weights[i][1:]
        hp = rmsnorm(xp, nw)
        hs = rmsnorm(xs, nw)
        if kind == 0:
            lam_init = 0.8 - 0.6 * math.exp(-0.3 * i)
            yp, pa, pb = diff_attn_mixer(hp, None, None, *mw, lam_init)
            ys, sa, sb = diff_attn_mixer(hs, *layer_state[i], *mw, lam_init)
        elif kind == 1:
            yp, pa, pb = mamba2_mixer(hp, None, None, *mw)
            ys, sa, sb = mamba2_mixer(hs, *layer_state[i], *mw)
        else:
            yp, pa, pb = s5_mixer(hp, None, None, *mw)
            ys, sa, sb = s5_mixer(hs, *layer_state[i], *mw)
        new_states.append((pa, pb, sa, sb))
        xp = xp + yp
        xs = xs + ys
    y_prompt = rmsnorm(xp, final_norm)
    y_sample = rmsnorm(xs, final_norm)
    k_p0, v_p0, k_s0, v_s0 = new_states[0]
    conv_p1, ssm_p1, conv_s1, ssm_s1 = new_states[1]
    s5re_p2, s5im_p2, s5re_s2, s5im_s2 = new_states[2]
    k_p3, v_p3, k_s3, v_s3 = new_states[3]
    return (y_prompt, y_sample, k_p0, v_p0, k_s0, v_s0, conv_p1, ssm_p1, conv_s1, ssm_s1,
            s5re_p2, s5im_p2, s5re_s2, s5im_s2, k_p3, v_p3, k_s3, v_s3)
```

```python
import dataclasses
import functools
import math

import jax
import jax.numpy as jnp
from jax import lax
from jax.experimental import pallas as pl
from jax.experimental.pallas import tpu as pltpu

F32 = jnp.float32
BF16 = jnp.bfloat16

LANES = 128
SUBLANES = 8
VMEM_LIMIT_BYTES = 48 * 1024 * 1024


@dataclasses.dataclass(frozen=True)
class Cfg:
    d_model: int = 4096
    seq: int = 16384
    dec_batch: int = 16
    dec_seq: int = 16
    past_len: int = 1024
    chunk: int = 64
    rms_eps: float = 1e-6
    rope_theta: float = 10000.0
    a_heads: int = 16
    a_head_dim: int = 128
    b_head_dim: int = 64
    b_groups: int = 8
    b_state: int = 128
    b_conv: int = 4
    c_group: int = 16
    c_state: int = 64
    attn_tq: int = 512
    attn_tk: int = 512
    ssd_chunk: int = 128
    s5_seg: int = 256
    s5_tt: int = 128
    mm_tm: int = 1024
    mm_tn: int = 512

    @property
    def a_vdim(self):
        return 2 * self.a_head_dim

    @property
    def a_width(self):
        return self.a_heads * self.a_vdim

    @property
    def b_width(self):
        return 2 * self.d_model

    @property
    def b_heads(self):
        return self.b_width // self.b_head_dim

    @property
    def b_conv_dim(self):
        return self.b_width + 2 * self.b_groups * self.b_state

    @property
    def c_width(self):
        return self.d_model

    @property
    def c_groups(self):
        return self.c_width // self.c_group


def _cp(sem):
    return pltpu.CompilerParams(dimension_semantics=sem, vmem_limit_bytes=VMEM_LIMIT_BYTES)


def _tile(n, pref, mult):
    if n <= pref:
        return n
    t = (pref // mult) * mult
    while t > mult and n % t:
        t -= mult
    assert n % t == 0, (n, pref, mult)
    return t


def _silu(x):
    return x / (1.0 + jnp.exp(-x))


def _rmsnorm_kernel(x_ref, w_ref, o_ref, *, eps):
    x = x_ref[...].astype(F32)
    ms = jnp.mean(x * x, axis=-1, keepdims=True)
    o_ref[...] = (x * lax.rsqrt(ms + eps) * w_ref[...]).astype(o_ref.dtype)


def _rmsnorm(x, w, out_dtype, eps, name):
    R, D = x.shape
    tr = _tile(R, 256, 8)
    return pl.pallas_call(
        functools.partial(_rmsnorm_kernel, eps=eps),
        grid=(R // tr,),
        in_specs=[pl.BlockSpec((tr, D), lambda i: (i, 0)), pl.BlockSpec((1, D), lambda i: (0, 0))],
        out_specs=pl.BlockSpec((tr, D), lambda i: (i, 0)),
        out_shape=jax.ShapeDtypeStruct((R, D), out_dtype),
        compiler_params=_cp(("parallel",)),
        name=name,
    )(x, w.reshape(1, D).astype(F32))


def _gated_rmsnorm_kernel(y_ref, z_ref, w_ref, o_ref, *, eps):
    y = y_ref[...].astype(F32) * _silu(z_ref[...].astype(F32))
    ms = jnp.mean(y * y, axis=-1, keepdims=True)
    o_ref[...] = (y * lax.rsqrt(ms + eps) * w_ref[...]).astype(o_ref.dtype)


def _gated_rmsnorm(y, z, w, eps, name):
    R, D = y.shape
    tr = _tile(R, 128, 8)
    return pl.pallas_call(
        functools.partial(_gated_rmsnorm_kernel, eps=eps),
        grid=(R // tr,),
        in_specs=[pl.BlockSpec((tr, D), lambda i: (i, 0)), pl.BlockSpec((tr, D), lambda i: (i, 0)),
                  pl.BlockSpec((1, D), lambda i: (0, 0))],
        out_specs=pl.BlockSpec((tr, D), lambda i: (i, 0)),
        out_shape=jax.ShapeDtypeStruct((R, D), BF16),
        compiler_params=_cp(("parallel",)),
        name=name,
    )(y, z, w.reshape(1, D).astype(F32))


def _mm_kernel(*refs, n_extra, epi):
    a_ref, w_ref = refs[0], refs[1]
    extras = refs[2:2 + n_extra]
    outs = refs[2 + n_extra:]
    acc = jnp.dot(a_ref[...], w_ref[...], preferred_element_type=F32)
    res = epi(acc, *[e[...] for e in extras])
    for o, r in zip(outs, res):
        o[...] = r.astype(o.dtype)


def _matmul(a, w, col0, ncols, out_dtypes, epi, extras, tm, tn, name):
    R, K = a.shape
    tm = _tile(R, tm, 8)
    tn = _tile(ncols, tn, LANES)
    assert col0 % tn == 0
    cb0 = col0 // tn
    in_specs = [pl.BlockSpec((tm, K), lambda i, j: (i, 0)),
                pl.BlockSpec((K, tn), lambda i, j: (0, cb0 + j))]
    args = [a, w]
    for arr, kind in extras:
        if kind == "row":
            in_specs.append(pl.BlockSpec((tm, arr.shape[1]), lambda i, j: (i, 0)))
        elif kind == "tile":
            in_specs.append(pl.BlockSpec((tm, tn), lambda i, j: (i, j)))
        elif kind == "col":
            in_specs.append(pl.BlockSpec((1, tn), lambda i, j: (0, j)))
        else:
            raise ValueError(kind)
        args.append(arr)
    out_specs = [pl.BlockSpec((tm, tn), lambda i, j: (i, j)) for _ in out_dtypes]
    out_shape = [jax.ShapeDtypeStruct((R, ncols), dt) for dt in out_dtypes]
    return pl.pallas_call(
        functools.partial(_mm_kernel, n_extra=len(extras), epi=epi),
        grid=(R // tm, ncols // tn),
        in_specs=in_specs, out_specs=out_specs, out_shape=out_shape,
        compiler_params=_cp(("parallel", "arbitrary")),
        name=name,
    )(*args)


def _epi_plain(acc):
    return (acc,)


def _epi_two(acc):
    return (acc, acc)


def _rope_tile(acc, cos, sin):
    parts = []
    for c in range(acc.shape[1] // LANES):
        x = acc[:, c * LANES:(c + 1) * LANES]
        parts.append(x * cos + pltpu.roll(x, LANES // 2, axis=1) * sin)
    return parts[0] if len(parts) == 1 else jnp.concatenate(parts, axis=1)


def _epi_rope_q(acc, cos, sin, *, scale):
    return (_rope_tile(acc, cos, sin) * scale,)


def _epi_rope_k(acc, cos, sin):
    r = _rope_tile(acc, cos, sin)
    return (r, r)


def _epi_residual(acc, x):
    return (x + acc,)


def _epi_softplus_bias(acc, b):
    v = acc + b
    return (jnp.maximum(v, 0.0) + jnp.log(1.0 + jnp.exp(-jnp.abs(v))),)


def _epi_glu(acc, b, yg, g):
    ygf = yg.astype(F32)
    sig = 1.0 / (1.0 + jnp.exp(-(acc + b)))
    return (ygf * sig * _silu(g.astype(F32)),)


def _rope_table_kernel(inv_ref, cos_ref, sin_ref, *, tr, pos0, period):
    i = pl.program_id(0)
    row = lax.broadcasted_iota(jnp.int32, (tr, LANES), 0) + i * tr
    pos = (pos0 + row % period).astype(F32)
    ang = pos * inv_ref[...]
    lane = lax.broadcasted_iota(jnp.int32, (tr, LANES), 1)
    cos_ref[...] = jnp.cos(ang)
    sin_ref[...] = jnp.where(lane < LANES // 2, -jnp.sin(ang), jnp.sin(ang))


def _rope_tables(cfg, rows, pos0, period, name):
    dh = cfg.a_head_dim
    inv = 1.0 / (cfg.rope_theta ** (jnp.arange(0, dh, 2, dtype=F32) / dh))
    inv2 = jnp.concatenate([inv, inv]).reshape(1, dh)
    tr = _tile(rows, 512, 8)
    return pl.pallas_call(
        functools.partial(_rope_table_kernel, tr=tr, pos0=pos0, period=period),
        grid=(rows // tr,),
        in_specs=[pl.BlockSpec((1, dh), lambda i: (0, 0))],
        out_specs=[pl.BlockSpec((tr, dh), lambda i: (i, 0))] * 2,
        out_shape=[jax.ShapeDtypeStruct((rows, dh), F32)] * 2,
        compiler_params=_cp(("parallel",)),
        name=name,
    )(inv2)


def _lambda(lamv, lam_init):
    s1 = jnp.sum(lamv[0:1, :] * lamv[1:2, :], axis=-1, keepdims=True)
    s2 = jnp.sum(lamv[2:3, :] * lamv[3:4, :], axis=-1, keepdims=True)
    return jnp.exp(s1) - jnp.exp(s2) + lam_init


def _attn_finish(o0, o1, lam, g, sw, lam_init, eps):
    o = o0 - lam * o1
    ms = jnp.mean(o * o, axis=-1, keepdims=True)
    on = (o * lax.rsqrt(ms + eps) * sw) * (1.0 - lam_init)
    return (on * _silu(g.astype(F32))).astype(BF16)


def _attn_prompt_kernel(lamv_ref, q_ref, k_ref, v_ref, g_ref, sw_ref, o_ref, m_sc, l_sc, acc_sc,
                        *, tq, tk, dh, chunk, lam_init, eps):
    qi = pl.program_id(1)
    q = q_ref[...]
    m_sc[...] = jnp.full(m_sc.shape, -jnp.inf, F32)
    l_sc[...] = jnp.zeros(l_sc.shape, F32)
    acc_sc[...] = jnp.zeros(acc_sc.shape, F32)

    def step(kb, masked):
        off = pl.multiple_of(kb * tk, tk)
        kblk = k_ref[pl.ds(off, tk), :]
        vblk = v_ref[pl.ds(off, tk), :]
        if masked:
            rows = qi * tq + lax.broadcasted_iota(jnp.int32, (tq, tk), 0)
            cols = off + lax.broadcasted_iota(jnp.int32, (tq, tk), 1)
            visible = (cols // chunk) <= (rows // chunk)
        for mp in range(2):
            s = lax.dot_general(q[:, mp * dh:(mp + 1) * dh], kblk[:, mp * dh:(mp + 1) * dh],
                                (((1,), (1,)), ((), ())), preferred_element_type=F32)
            if masked:
                s = jnp.where(visible, s, -jnp.inf)
            m_prev = m_sc[mp]
            m_new = jnp.maximum(m_prev, jnp.max(s, axis=-1, keepdims=True))
            alpha = jnp.exp(m_prev - m_new)
            p = jnp.exp(s - m_new)
            l_sc[mp] = alpha * l_sc[mp] + jnp.sum(p, axis=-1, keepdims=True)
            acc_sc[mp] = alpha * acc_sc[mp] + jnp.dot(p.astype(BF16), vblk, preferred_element_type=F32)
            m_sc[mp] = m_new

    nfull = qi * (tq // tk)

    def body(kb, c):
        step(kb, False)
        return c

    lax.fori_loop(0, nfull, body, 0)
    for d in range(tq // tk):
        step(nfull + d, True)

    lam = _lambda(lamv_ref[...], lam_init)
    o0 = acc_sc[0] / l_sc[0]
    o1 = acc_sc[1] / l_sc[1]
    o_ref[...] = _attn_finish(o0, o1, lam, g_ref[...], sw_ref[...], lam_init, eps)


def _attn_prompt(cfg, q, k, v, g, lamv, sw, lam_init, name):
    L = q.shape[0]
    H, dv, dh = cfg.a_heads, cfg.a_vdim, cfg.a_head_dim
    tq = _tile(L, cfg.attn_tq, cfg.chunk)
    tk = _tile(tq, cfg.attn_tk, cfg.chunk)
    kern = functools.partial(_attn_prompt_kernel, tq=tq, tk=tk, dh=dh, chunk=cfg.chunk,
                             lam_init=lam_init, eps=cfg.rms_eps)
    return pl.pallas_call(
        kern,
        grid=(H, L // tq),
        in_specs=[pl.BlockSpec((4, dh), lambda h, i: (0, 0)),
                  pl.BlockSpec((tq, dv), lambda h, i: (i, h)),
                  pl.BlockSpec((L, dv), lambda h, i: (0, h)),
                  pl.BlockSpec((L, dv), lambda h, i: (0, h)),
                  pl.BlockSpec((tq, dv), lambda h, i: (i, h)),
                  pl.BlockSpec((1, dv), lambda h, i: (0, 0))],
        out_specs=pl.BlockSpec((tq, dv), lambda h, i: (i, h)),
        out_shape=jax.ShapeDtypeStruct((L, H * dv), BF16),
        scratch_shapes=[pltpu.VMEM((2, tq, 1), F32), pltpu.VMEM((2, tq, 1), F32),
                        pltpu.VMEM((2, tq, dv), F32)],
        compiler_params=_cp(("parallel", "arbitrary")),
        name=name,
    )(lamv, q, k, v, g, sw)


def _attn_sample_kernel(lamv_ref, q_ref, kn_ref, vn_ref, kc_ref, vc_ref, g_ref, sw_ref, o_ref,
                        *, P, Ls, dh, chunk, lam_init, eps):
    q = q_ref[...]
    kn = kn_ref[...]
    vn = vn_ref[...]
    kc = kc_ref[0].astype(BF16)
    vc = vc_ref[0].astype(BF16)
    qpos_p = P + lax.broadcasted_iota(jnp.int32, (Ls, P), 0)
    kpos_p = lax.broadcasted_iota(jnp.int32, (Ls, P), 1)
    vis_p = (kpos_p // chunk) <= (qpos_p // chunk)
    qpos_n = P + lax.broadcasted_iota(jnp.int32, (Ls, Ls), 0)
    kpos_n = P + lax.broadcasted_iota(jnp.int32, (Ls, Ls), 1)
    vis_n = (kpos_n // chunk) <= (qpos_n // chunk)
    outs = []
    dn = (((1,), (1,)), ((), ()))
    for mp in range(2):
        qm = q[:, mp * dh:(mp + 1) * dh]
        sp = lax.dot_general(qm, kc[:, mp * dh:(mp + 1) * dh], dn, preferred_element_type=F32)
        sn = lax.dot_general(qm, kn[:, mp * dh:(mp + 1) * dh], dn, preferred_element_type=F32)
        sp = jnp.where(vis_p, sp, -jnp.inf)
        sn = jnp.where(vis_n, sn, -jnp.inf)
        m = jnp.maximum(jnp.max(sp, axis=-1, keepdims=True), jnp.max(sn, axis=-1, keepdims=True))
        pp = jnp.exp(sp - m)
        pn = jnp.exp(sn - m)
        l = jnp.sum(pp, axis=-1, keepdims=True) + jnp.sum(pn, axis=-1, keepdims=True)
        acc = (jnp.dot(pp.astype(BF16), vc, preferred_element_type=F32)
               + jnp.dot(pn.astype(BF16), vn, preferred_element_type=F32))
        outs.append(acc / l)
    lam = _lambda(lamv_ref[...], lam_init)
    o_ref[...] = _attn_finish(outs[0], outs[1], lam, g_ref[...], sw_ref[...], lam_init, eps)


def _attn_sample(cfg, q, k, v, g, kc, vc, lamv, sw, lam_init, name):
    B, Ls, P = cfg.dec_batch, cfg.dec_seq, cfg.past_len
    H, dv, dh = cfg.a_heads, cfg.a_vdim, cfg.a_head_dim
    kern = functools.partial(_attn_sample_kernel, P=P, Ls=Ls, dh=dh, chunk=cfg.chunk,
                             lam_init=lam_init, eps=cfg.rms_eps)
    row = pl.BlockSpec((Ls, dv), lambda b, h: (b, h))
    cache = pl.BlockSpec((1, P, dv), lambda b, h: (b, 0, h))
    return pl.pallas_call(
        kern,
        grid=(B, H),
        in_specs=[pl.BlockSpec((4, dh), lambda b, h: (0, 0)), row, row, row, cache, cache, row,
                  pl.BlockSpec((1, dv), lambda b, h: (0, 0))],
        out_specs=row,
        out_shape=jax.ShapeDtypeStruct((B * Ls, H * dv), BF16),
        compiler_params=_cp(("parallel", "parallel")),
        name=name,
    )(lamv, q, k, v, kc.reshape(B, P, H * dv), vc.reshape(B, P, H * dv), g, sw)


def _attn_layer(cfg, li, xp, xs, cache_k, cache_v, nw, w_in, lq1, lk1, lq2, lk2, subln, w_out, ropes):
    lam_init = 0.8 - 0.6 * math.exp(-0.3 * li)
    W = cfg.a_width
    w_in_b = w_in.astype(BF16)
    w_out_b = w_out.astype(BF16)
    lamv = jnp.stack([lq1, lk1, lq2, lk2]).astype(F32)
    sw = subln.reshape(1, cfg.a_vdim).astype(F32)
    scale = cfg.a_head_dim ** -0.5
    res = []
    for tag, x, (cos, sin) in (("p", xp, ropes[0]), ("s", xs, ropes[1])):
        nm = f"l{li}{tag}"
        h = _rmsnorm(x, nw, BF16, cfg.rms_eps, nm + "_norm")
        tm, tn = cfg.mm_tm, cfg.mm_tn
        rope_ex = [(cos, "row"), (sin, "row")]
        (q,) = _matmul(h, w_in_b, 0, W, [BF16], functools.partial(_epi_rope_q, scale=scale), rope_ex,
                       tm, tn, nm + "_q")
        k32, k16 = _matmul(h, w_in_b, W, W, [F32, BF16], _epi_rope_k, rope_ex, tm, tn, nm + "_k")
        v32, v16 = _matmul(h, w_in_b, 2 * W, W, [F32, BF16], _epi_two, [], tm, tn, nm + "_v")
        (g,) = _matmul(h, w_in_b, 3 * W, W, [BF16], _epi_plain, [], tm, tn, nm + "_g")
        if tag == "p":
            y = _attn_prompt(cfg, q, k16, v16, g, lamv, sw, lam_init, nm + "_attn")
        else:
            y = _attn_sample(cfg, q, k16, v16, g, cache_k, cache_v, lamv, sw, lam_init, nm + "_attn")
        (xn,) = _matmul(y, w_out_b, 0, cfg.d_model, [F32], _epi_residual, [(x, "tile")], tm, tn, nm + "_out")
        res.append((xn, k32, v32))
    return res


def _conv_kernel(x_ref, prev_ref, w_ref, b_ref, o_ref, carry_sc, *, tr, taps):
    r = pl.program_id(2)

    @pl.when(r == 0)
    def _():
        carry_sc[...] = prev_ref[0]

    x = x_ref[...].astype(F32)
    w = w_ref[...]
    b = b_ref[...]
    acc = b + x * w[taps - 1:taps, :]
    for j in range(1, taps):
        acc = acc + pltpu.roll(x, j, axis=0) * w[taps - 1 - j:taps - j, :]
    o_ref[...] = _silu(acc).astype(o_ref.dtype)
    x8 = x[0:SUBLANES, :]
    pr = carry_sc[...]
    row8 = lax.broadcasted_iota(jnp.int32, x8.shape, 0)
    acc8 = b + x8 * w[taps - 1:taps, :]
    for j in range(1, taps):
        sh = jnp.where(row8 < j, pltpu.roll(pr, j, axis=0), pltpu.roll(x8, j, axis=0))
        acc8 = acc8 + sh * w[taps - 1 - j:taps - j, :]
    o_ref[0:SUBLANES, :] = _silu(acc8).astype(o_ref.dtype)
    carry_sc[...] = x[tr - SUBLANES:tr, :]


def _conv_silu(xbc, prev8, w_t, b, nseq, name):
    R, C = xbc.shape
    taps = w_t.shape[0]
    Lseq = R // nseq
    tr = _tile(Lseq, 512, 16)
    tc = _tile(C, 512, LANES)
    nr = Lseq // tr
    return pl.pallas_call(
        functools.partial(_conv_kernel, tr=tr, taps=taps),
        grid=(C // tc, nseq, nr),
        in_specs=[pl.BlockSpec((tr, tc), lambda c, s, r: (s * nr + r, c)),
                  pl.BlockSpec((1, SUBLANES, tc), lambda c, s, r: (s, 0, c)),
                  pl.BlockSpec((taps, tc), lambda c, s, r: (0, c)),
                  pl.BlockSpec((1, tc), lambda c, s, r: (0, c))],
        out_specs=pl.BlockSpec((tr, tc), lambda c, s, r: (s * nr + r, c)),
        out_shape=jax.ShapeDtypeStruct((R, C), BF16),
        scratch_shapes=[pltpu.VMEM((SUBLANES, tc), F32)],
        compiler_params=_cp(("parallel", "arbitrary", "arbitrary")),
        name=name,
    )(xbc, prev8, w_t, b)


def _split_bf16(v):
    hi = v.astype(BF16)
    lo = (v - hi.astype(F32)).astype(BF16)
    return hi, lo


def _expand(v, e):
    hi, lo = _split_bf16(v)
    return (jnp.dot(hi, e, preferred_element_type=F32) + jnp.dot(lo, e, preferred_element_type=F32))


def _ssd_kernel(*refs, T, gw, hg, hd, N, has_init):
    if has_init:
        x_ref, b_ref, c_ref, dt_ref, alog_ref, d_ref, e_ref, init_ref, y_ref, st_ref, s_sc = refs
    else:
        x_ref, b_ref, c_ref, dt_ref, alog_ref, d_ref, e_ref, y_ref, st_ref, s_sc = refs
    ci = pl.program_id(2)

    @pl.when(ci == 0)
    def _():
        if has_init:
            s_sc[...] = init_ref[0].reshape(gw, N).T
        else:
            s_sc[...] = jnp.zeros(s_sc.shape, F32)

    dt = dt_ref[...]
    a_neg = -jnp.exp(alog_ref[...])
    a = dt * a_neg
    row = lax.broadcasted_iota(jnp.int32, a.shape, 0)
    d = 1
    while d < T:
        a = a + jnp.where(row >= d, pltpu.roll(a, d, axis=0), 0.0)
        d *= 2
    a_last = a[T - 1:T, :]
    e = e_ref[...]
    dt_x = _expand(dt, e)
    expa_x = _expand(jnp.exp(a), e)
    decay_x = _expand(jnp.exp(a_last - a), e)
    x = x_ref[...].astype(F32)
    xdt = x * dt_x
    xdt_b = xdt.astype(BF16)
    bm = b_ref[...]
    cm = c_ref[...]
    cb = lax.dot_general(cm, bm, (((1,), (1,)), ((), ())), preferred_element_type=F32)
    s0 = s_sc[...]
    y_off = jnp.dot(cm, s0.astype(BF16), preferred_element_type=F32) * expa_x
    a_t = a.T
    tri = (lax.broadcasted_iota(jnp.int32, (T, T), 0) >= lax.broadcasted_iota(jnp.int32, (T, T), 1))
    lane = lax.broadcasted_iota(jnp.int32, (T, LANES), 1)
    per_pair = LANES // hd
    pieces = []
    for pi in range(gw // LANES):
        xp = xdt_b[:, pi * LANES:(pi + 1) * LANES]
        piece = None
        for e_i in range(per_pair):
            hl = pi * per_pair + e_i
            seg = a[:, hl:hl + 1] - a_t[hl:hl + 1, :]
            m = (cb * jnp.where(tri, jnp.exp(seg), 0.0)).astype(BF16)
            r = jnp.dot(m, xp, preferred_element_type=F32)
            if piece is None:
                piece = r
            else:
                piece = jnp.where(lane < e_i * hd, piece, r)
        pieces.append(piece)
    y_diag = pieces[0] if len(pieces) == 1 else jnp.concatenate(pieces, axis=1)
    y_ref[...] = (y_diag + y_off + d_ref[...] * x).astype(y_ref.dtype)
    xw = (xdt * decay_x).astype(BF16)
    upd = lax.dot_general(bm, xw, (((0,), (0,)), ((), ())), preferred_element_type=F32)
    s_new = s0 * expa_x[T - 1:T, :] + upd
    s_sc[...] = s_new

    @pl.when(ci == pl.num_programs(2) - 1)
    def _():
        st_ref[0] = s_new.T.reshape(hg, hd, N)


def _ssd(cfg, xbc_act, dtg, alog_g, d_exp, e_mat, init, nseq, T, name):
    R = xbc_act.shape[0]
    G, N, hd = cfg.b_groups, cfg.b_state, cfg.b_head_dim
    hg = cfg.b_heads // G
    gw = hg * hd
    Lseq = R // nseq
    nc = Lseq // T
    bw = cfg.b_width
    boff = bw // N
    has_init = init is not None
    kern = functools.partial(_ssd_kernel, T=T, gw=gw, hg=hg, hd=hd, N=N, has_init=has_init)
    in_specs = [pl.BlockSpec((T, gw), lambda s, g, c: (s * nc + c, g)),
                pl.BlockSpec((T, N), lambda s, g, c: (s * nc + c, boff + g)),
                pl.BlockSpec((T, N), lambda s, g, c: (s * nc + c, boff + G + g)),
                pl.BlockSpec((T, LANES), lambda s, g, c: (s * nc + c, g)),
                pl.BlockSpec((1, LANES), lambda s, g, c: (0, g)),
                pl.BlockSpec((1, gw), lambda s, g, c: (0, g)),
                pl.BlockSpec((LANES, gw), lambda s, g, c: (0, 0))]
    args = [xbc_act, xbc_act, xbc_act, dtg, alog_g, d_exp, e_mat]
    if has_init:
        in_specs.append(pl.BlockSpec((1, hg, hd, N), lambda s, g, c: (s, g, 0, 0)))
        args.append(init)
    return pl.pallas_call(
        kern,
        grid=(nseq, G, nc),
        in_specs=in_specs,
        out_specs=[pl.BlockSpec((T, gw), lambda s, g, c: (s * nc + c, g)),
                   pl.BlockSpec((1, hg, hd, N), lambda s, g, c: (s, g, 0, 0))],
        out_shape=[jax.ShapeDtypeStruct((R, bw), BF16),
                   jax.ShapeDtypeStruct((nseq, cfg.b_heads, hd, N), F32)],
        scratch_shapes=[pltpu.VMEM((N, gw), F32)],
        compiler_params=_cp(("parallel", "parallel", "arbitrary")),
        name=name,
    )(*args)


def _ssd_layer(cfg, li, xp, xs, conv_prev, ssm_prev, nw, w_in, conv_w, conv_b, dt_bias, a_log, d_skip,
               gnorm, w_out):
    bw, cd, H, G = cfg.b_width, cfg.b_conv_dim, cfg.b_heads, cfg.b_groups
    hg = H // G
    hd = cfg.b_head_dim
    gw = hg * hd
    w_in_b = w_in.astype(BF16)
    w_out_b = w_out.astype(BF16)

    def per_group(v):
        lead = v.shape[:-1]
        v = v.reshape(lead + (G, hg))
        v = jnp.pad(v, [(0, 0)] * len(lead) + [(0, 0), (0, LANES - hg)])
        return v.reshape(lead + (G * LANES,))

    w_dt_g = per_group(w_in[:, bw + cd:]).astype(BF16)
    dtb_g = per_group(dt_bias.astype(F32)).reshape(1, G * LANES)
    alog_g = per_group(a_log.astype(F32)).reshape(1, G * LANES)
    d_exp = jnp.repeat(d_skip.astype(F32), hd).reshape(1, bw)
    e_mat = (jnp.arange(LANES)[:, None] == (jnp.arange(gw)[None, :] // hd)).astype(BF16)
    w_t = conv_w.astype(F32).T
    cb2 = conv_b.astype(F32).reshape(1, cd)
    T = cfg.ssd_chunk
    B, Ls = cfg.dec_batch, cfg.dec_seq
    res = []
    for tag, x in (("p", xp), ("s", xs)):
        nm = f"l{li}{tag}"
        tm, tn = cfg.mm_tm, cfg.mm_tn
        h = _rmsnorm(x, nw, BF16, cfg.rms_eps, nm + "_norm")
        (z,) = _matmul(h, w_in_b, 0, bw, [BF16], _epi_plain, [], tm, tn, nm + "_z")
        (xbc,) = _matmul(h, w_in_b, bw, cd, [BF16], _epi_plain, [], tm, tn, nm + "_xbc")
        (dtg,) = _matmul(h, w_dt_g, 0, G * LANES, [F32], _epi_softplus_bias, [(dtb_g, "col")], tm, tn,
                         nm + "_dt")
        if tag == "p":
            prev8 = jnp.zeros((1, SUBLANES, cd), F32)
            act = _conv_silu(xbc, prev8, w_t, cb2, 1, nm + "_conv")
            conv_new = xbc[x.shape[0] - (cfg.b_conv - 1):].astype(F32)[None]
            y, st = _ssd(cfg, act, dtg, alog_g, d_exp, e_mat, None, 1, T, nm + "_ssd")
        else:
            prev8 = jnp.pad(conv_prev.astype(F32), ((0, 0), (SUBLANES - (cfg.b_conv - 1), 0), (0, 0)))
            act = _conv_silu(xbc, prev8, w_t, cb2, B, nm + "_conv")
            conv_new = xbc.reshape(B, Ls, cd)[:, Ls - (cfg.b_conv - 1):].astype(F32)
            act_p = jnp.pad(act.reshape(B, Ls, cd), ((0, 0), (0, T - Ls), (0, 0))).reshape(B * T, cd)
            dt_p = jnp.pad(dtg.reshape(B, Ls, G * LANES), ((0, 0), (0, T - Ls), (0, 0))).reshape(B * T, G * LANES)
            y_p, st = _ssd(cfg, act_p, dt_p, alog_g, d_exp, e_mat, ssm_prev.astype(F32), B, T, nm + "_ssd")
            y = y_p.reshape(B, T, bw)[:, :Ls].reshape(B * Ls, bw)
        yn = _gated_rmsnorm(y, z, gnorm, cfg.rms_eps, nm + "_gnorm")
        (xn,) = _matmul(yn, w_out_b, 0, cfg.d_model, [F32], _epi_residual, [(x, "tile")], tm // 2, tn,
                        nm + "_out")
        res.append((xn, conv_new, st))
    return res


def _s5_disc_kernel(lr_ref, li_ref, ls_ref, br_ref, bi_ref, ar_ref, ai_ref, bbr_ref, bbi_ref):
    lr = lr_ref[...]
    li = li_ref[...]
    step = jnp.exp(ls_ref[...])
    mag = jnp.exp(lr * step)
    ar = mag * jnp.cos(li * step)
    ai = mag * jnp.sin(li * step)
    den = lr * lr + li * li
    nr = ar - 1.0
    ni = ai
    cr = ((nr * lr + ni * li) / den)[:, None, :]
    ci = ((ni * lr - nr * li) / den)[:, None, :]
    br = br_ref[...]
    bi = bi_ref[...]
    ar_ref[...] = ar
    ai_ref[...] = ai
    bbr_ref[...] = cr * br - ci * bi
    bbi_ref[...] = cr * bi + ci * br


def _s5_discretize(cfg, lam_re, lam_im, log_step, b_re, b_im, name):
    G, P, K = cfg.c_groups, cfg.c_state, cfg.c_group
    bt_r = jnp.transpose(b_re.astype(F32), (0, 2, 1))
    bt_i = jnp.transpose(b_im.astype(F32), (0, 2, 1))
    tg = _tile(G, 64, 8)
    s2 = pl.BlockSpec((tg, P), lambda i: (i, 0))
    s3 = pl.BlockSpec((tg, K, P), lambda i: (i, 0, 0))
    return pl.pallas_call(
        _s5_disc_kernel,
        grid=(G // tg,),
        in_specs=[s2, s2, pl.BlockSpec((tg, 1), lambda i: (i, 0)), s3, s3],
        out_specs=[s2, s2, s3, s3],
        out_shape=[jax.ShapeDtypeStruct((G, P), F32)] * 2 + [jax.ShapeDtypeStruct((G, K, P), F32)] * 2,
        compiler_params=_cp(("parallel",)),
        name=name,
    )(lam_re.astype(F32), lam_im.astype(F32), log_step.astype(F32).reshape(G, 1), bt_r, bt_i)


def _gelu_tanh(v):
    return 0.5 * v * (1.0 + jnp.tanh(math.sqrt(2.0 / math.pi) * (v + 0.044715 * (v * v * v))))


def _s5_scan_kernel(*refs, tT, sw, emit_y):
    if emit_y:
        (u_ref, bbd_ref, cbd_ref, ar_ref, ai_ref, d_ref, hr0_ref, hi0_ref,
         yg_ref, hrT_ref, hiT_ref, bu_sc, hs_sc, hr_sc, hi_sc) = refs
    else:
        (u_ref, bbd_ref, ar_ref, ai_ref, hr0_ref, hi0_ref,
         hrT_ref, hiT_ref, bu_sc, hr_sc, hi_sc) = refs
    tb = pl.program_id(2)

    @pl.when(tb == 0)
    def _():
        hr_sc[...] = hr0_ref[...]
        hi_sc[...] = hi0_ref[...]

    ub = u_ref[...].reshape(SUBLANES * tT, LANES)
    bu = jnp.dot(ub, bbd_ref[0], preferred_element_type=F32)
    nl = sw // LANES
    for l in range(2 * nl):
        bu_sc[l] = bu[:, l * LANES:(l + 1) * LANES]
    ar = jnp.broadcast_to(ar_ref[...], (SUBLANES, sw))
    ai = jnp.broadcast_to(ai_ref[...], (SUBLANES, sw))

    def body(t, carry):
        hr, hi = carry
        rows = pl.ds(t, SUBLANES, stride=tT)
        bur = jnp.concatenate([bu_sc[l, rows, :] for l in range(nl)], axis=1)
        bui = jnp.concatenate([bu_sc[nl + l, rows, :] for l in range(nl)], axis=1)
        nr = ar * hr - ai * hi + bur
        ni = ar * hi + ai * hr + bui
        if emit_y:
            for l in range(nl):
                hs_sc[l, rows, :] = nr[:, l * LANES:(l + 1) * LANES]
                hs_sc[nl + l, rows, :] = ni[:, l * LANES:(l + 1) * LANES]
        return nr, ni

    hr, hi = lax.fori_loop(0, tT, body, (hr_sc[...], hi_sc[...]), unroll=4)
    hr_sc[...] = hr
    hi_sc[...] = hi
    if emit_y:
        hs = jnp.concatenate([hs_sc[l].astype(BF16) for l in range(2 * nl)], axis=1)
        y = jnp.dot(hs, cbd_ref[0], preferred_element_type=F32)
        v = y + d_ref[...] * ub.astype(F32)
        yg_ref[...] = _gelu_tanh(v).reshape(SUBLANES, tT, LANES).astype(yg_ref.dtype)

    @pl.when(tb == pl.num_programs(2) - 1)
    def _():
        hrT_ref[...] = hr
        hiT_ref[...] = hi


def _s5_scan(cfg, u3, bbd, cbd, ar, ai, d2, hr0, hi0, emit_y, name):
    NS, T, C = u3.shape
    nj = C // LANES
    sw = bbd.shape[2] // 2
    tT = _tile(T, cfg.s5_tt, 16)
    nsg = NS // SUBLANES
    kern = functools.partial(_s5_scan_kernel, tT=tT, sw=sw, emit_y=emit_y)
    u_spec = pl.BlockSpec((SUBLANES, tT, LANES), lambda j, s, t: (s, t, j))
    bbd_spec = pl.BlockSpec((1, LANES, 2 * sw), lambda j, s, t: (j, 0, 0))
    a_spec = pl.BlockSpec((1, sw), lambda j, s, t: (0, j))
    st_spec = pl.BlockSpec((SUBLANES, sw), lambda j, s, t: (s, j))
    st_shape = jax.ShapeDtypeStruct((NS, nj * sw), F32)
    scratch = [pltpu.VMEM((2 * sw // LANES, SUBLANES * tT, LANES), F32)]
    if emit_y:
        in_specs = [u_spec, bbd_spec, pl.BlockSpec((1, 2 * sw, LANES), lambda j, s, t: (j, 0, 0)),
                    a_spec, a_spec, pl.BlockSpec((1, LANES), lambda j, s, t: (0, j)), st_spec, st_spec]
        args = [u3, bbd, cbd, ar, ai, d2, hr0, hi0]
        out_specs = [u_spec, st_spec, st_spec]
        out_shape = [jax.ShapeDtypeStruct((NS, T, C), BF16), st_shape, st_shape]
        scratch.append(pltpu.VMEM((2 * sw // LANES, SUBLANES * tT, LANES), F32))
    else:
        in_specs = [u_spec, bbd_spec, a_spec, a_spec, st_spec, st_spec]
        args = [u3, bbd, ar, ai, hr0, hi0]
        out_specs = [st_spec, st_spec]
        out_shape = [st_shape, st_shape]
    scratch += [pltpu.VMEM((SUBLANES, sw), F32)] * 2
    return pl.pallas_call(
        kern,
        grid=(nj, nsg, T // tT),
        in_specs=in_specs, out_specs=out_specs, out_shape=out_shape,
        scratch_shapes=scratch,
        compiler_params=_cp(("parallel", "parallel", "arbitrary")),
        name=name,
    )(*args)


def _s5_combine_kernel(er_ref, ei_ref, ar_ref, ai_ref, h0r_ref, h0i_ref, hr_ref, hi_ref, *, nseg, nsq):
    pr = ar_ref[...]
    pi_ = ai_ref[...]
    for _ in range(nsq):
        pr, pi_ = pr * pr - pi_ * pi_, 2.0 * pr * pi_
    hr = h0r_ref[...]
    hi = h0i_ref[...]
    hr_ref[0:1, :] = hr
    hi_ref[0:1, :] = hi
    for i in range(1, nseg):
        er = er_ref[i - 1:i, :]
        ei = ei_ref[i - 1:i, :]
        hr, hi = pr * hr - pi_ * hi + er, pr * hi + pi_ * hr + ei
        hr_ref[i:i + 1, :] = hr
        hi_ref[i:i + 1, :] = hi


def _s5_combine(er, ei, ar, ai, h0r, h0i, seg_len, name):
    nseg, W = er.shape
    nsq = int(round(math.log2(seg_len)))
    assert 2 ** nsq == seg_len
    tw = _tile(W, 1024, LANES)
    full = pl.BlockSpec((nseg, tw), lambda i: (0, i))
    row = pl.BlockSpec((1, tw), lambda i: (0, i))
    return pl.pallas_call(
        functools.partial(_s5_combine_kernel, nseg=nseg, nsq=nsq),
        grid=(W // tw,),
        in_specs=[full, full, row, row, row, row],
        out_specs=[full, full],
        out_shape=[jax.ShapeDtypeStruct((nseg, W), F32)] * 2,
        compiler_params=_cp(("parallel",)),
        name=name,
    )(er, ei, ar, ai, h0r, h0i)


def _s5_layer(cfg, li, xp, xs, s_re, s_im, nw, w_in, lam_re, lam_im, log_step, b_re, b_im, c_re, c_im,
              d_skip, w_glu, b_glu, w_out):
    CW, G, P, K = cfg.c_width, cfg.c_groups, cfg.c_state, cfg.c_group
    gpb = LANES // K
    nj = CW // LANES
    sw = gpb * P
    w_in_b = w_in.astype(BF16)
    w_glu_b = w_glu.astype(BF16)
    w_out_b = w_out.astype(BF16)
    ar, ai, bbr, bbi = _s5_discretize(cfg, lam_re, lam_im, log_step, b_re, b_im, f"l{li}_disc")
    eye = jnp.eye(gpb, dtype=F32)
    bb = jnp.stack([bbr, bbi]).reshape(2, nj, gpb, K, P)
    bbd = jnp.einsum("rjgkp,gh->jgkrhp", bb, eye).reshape(nj, LANES, 2 * sw).astype(BF16)
    cc = jnp.stack([c_re.astype(F32), -c_im.astype(F32)]).reshape(2, nj, gpb, K, P)
    cbd = jnp.einsum("rjgkp,gh->jrgphk", cc, eye).reshape(nj, 2 * sw, LANES).astype(BF16)
    ar2 = ar.reshape(1, G * P)
    ai2 = ai.reshape(1, G * P)
    d2 = d_skip.astype(F32).reshape(1, CW)
    bg2 = b_glu.astype(F32).reshape(1, CW)
    B, Ls = cfg.dec_batch, cfg.dec_seq
    res = []
    for tag, x in (("p", xp), ("s", xs)):
        nm = f"l{li}{tag}"
        tm, tn = cfg.mm_tm, cfg.mm_tn
        h = _rmsnorm(x, nw, BF16, cfg.rms_eps, nm + "_norm")
        (u,) = _matmul(h, w_in_b, 0, CW, [BF16], _epi_plain, [], tm, tn, nm + "_u")
        (g,) = _matmul(h, w_in_b, CW, CW, [BF16], _epi_plain, [], tm, tn, nm + "_g")
        if tag == "p":
            L = x.shape[0]
            seg = _tile(L // SUBLANES, cfg.s5_seg, 16)
            nseg = L // seg
            u3 = u.reshape(nseg, seg, CW)
            zs = jnp.zeros((nseg, G * P), F32)
            er, ei = _s5_scan(cfg, u3, bbd, None, ar2, ai2, None, zs, zs, False, nm + "_scan_a")
            z1 = jnp.zeros((1, G * P), F32)
            h0r, h0i = _s5_combine(er, ei, ar2, ai2, z1, z1, seg, nm + "_comb")
            yg3, hrT, hiT = _s5_scan(cfg, u3, bbd, cbd, ar2, ai2, d2, h0r, h0i, True, nm + "_scan_b")
            yg = yg3.reshape(L, CW)
            st_r = hrT[nseg - 1:nseg].reshape(1, G, P)
            st_i = hiT[nseg - 1:nseg].reshape(1, G, P)
        else:
            u3 = u.reshape(B, Ls, CW)
            yg3, hrT, hiT = _s5_scan(cfg, u3, bbd, cbd, ar2, ai2, d2,
                                     s_re.astype(F32).reshape(B, G * P), s_im.astype(F32).reshape(B, G * P),
                                     True, nm + "_scan")
            yg = yg3.reshape(B * Ls, CW)
            st_r = hrT.reshape(B, G, P)
            st_i = hiT.reshape(B, G, P)
        (y2,) = _matmul(yg, w_glu_b, 0, CW, [BF16], _epi_glu, [(bg2, "col"), (yg, "tile"), (g, "tile")],
                        tm, tn, nm + "_glu")
        (xn,) = _matmul(y2, w_out_b, 0, cfg.d_model, [F32], _epi_residual, [(x, "tile")], tm, tn, nm + "_out")
        res.append((xn, st_r, st_i))
    return res


def _forward(cfg, x_prompt, x_sample,
             cache_k_l0, cache_v_l0, state_conv_l1, state_ssm_l1,
             state_s5_re_l2, state_s5_im_l2, cache_k_l3, cache_v_l3,
             norm_l0, attn_w_in_l0, lam_q1_l0, lam_k1_l0, lam_q2_l0, lam_k2_l0, subln_l0, attn_w_out_l0,
             norm_l1, ssm_w_in_l1, conv_w_l1, conv_b_l1, dt_bias_l1, a_log_l1, d_l1, gnorm_l1, ssm_w_out_l1,
             norm_l2, s5_w_in_l2, lam_re_l2, lam_im_l2, log_step_l2, b_re_l2, b_im_l2, c_re_l2, c_im_l2,
             d_l2, glu_w_l2, glu_b_l2, s5_w_out_l2,
             norm_l3, attn_w_in_l3, lam_q1_l3, lam_k1_l3, lam_q2_l3, lam_k2_l3, subln_l3, attn_w_out_l3,
             final_norm):
    D = cfg.d_model
    L, B, Ls, P = cfg.seq, cfg.dec_batch, cfg.dec_seq, cfg.past_len
    H, dv = cfg.a_heads, cfg.a_vdim
    xp = x_prompt.reshape(L, D).astype(F32)
    xs = x_sample.reshape(B * Ls, D).astype(F32)
    ropes = (_rope_tables(cfg, L, 0, L, "rope_p"), _rope_tables(cfg, B * Ls, P, Ls, "rope_s"))

    (xp, kp0, vp0), (xs, ks0, vs0) = _attn_layer(
        cfg, 0, xp, xs, cache_k_l0, cache_v_l0, norm_l0, attn_w_in_l0, lam_q1_l0, lam_k1_l0, lam_q2_l0,
        lam_k2_l0, subln_l0, attn_w_out_l0, ropes)
    (xp, conv_p1, ssm_p1), (xs, conv_s1, ssm_s1) = _ssd_layer(
        cfg, 1, xp, xs, state_conv_l1, state_ssm_l1, norm_l1, ssm_w_in_l1, conv_w_l1, conv_b_l1, dt_bias_l1,
        a_log_l1, d_l1, gnorm_l1, ssm_w_out_l1)
    (xp, s5re_p2, s5im_p2), (xs, s5re_s2, s5im_s2) = _s5_layer(
        cfg, 2, xp, xs, state_s5_re_l2, state_s5_im_l2, norm_l2, s5_w_in_l2, lam_re_l2, lam_im_l2,
        log_step_l2, b_re_l2, b_im_l2, c_re_l2, c_im_l2, d_l2, glu_w_l2, glu_b_l2, s5_w_out_l2)
    (xp, kp3, vp3), (xs, ks3, vs3) = _attn_layer(
        cfg, 3, xp, xs, cache_k_l3, cache_v_l3, norm_l3, attn_w_in_l3, lam_q1_l3, lam_k1_l3, lam_q2_l3,
        lam_k2_l3, subln_l3, attn_w_out_l3, ropes)

    y_prompt = _rmsnorm(xp, final_norm, F32, cfg.rms_eps, "final_p").reshape(1, L, D)
    y_sample = _rmsnorm(xs, final_norm, F32, cfg.rms_eps, "final_s").reshape(B, Ls, D)

    def kv_p(a):
        return a.reshape(1, L, H, dv)

    def kv_s(a):
        return a.reshape(B, Ls, H, dv)

    return (y_prompt, y_sample, kv_p(kp0), kv_p(vp0), kv_s(ks0), kv_s(vs0),
            conv_p1, ssm_p1, conv_s1, ssm_s1, s5re_p2, s5im_p2, s5re_s2, s5im_s2,
            kv_p(kp3), kv_p(vp3), kv_s(ks3), kv_s(vs3))


_CFG = Cfg()


def kernel(x_prompt, x_sample, cache_k_l0, cache_v_l0, state_conv_l1, state_ssm_l1, state_s5_re_l2, state_s5_im_l2, cache_k_l3, cache_v_l3, norm_l0, attn_w_in_l0, lam_q1_l0, lam_k1_l0, lam_q2_l0, lam_k2_l0, subln_l0, attn_w_out_l0, norm_l1, ssm_w_in_l1, conv_w_l1, conv_b_l1, dt_bias_l1, a_log_l1, d_l1, gnorm_l1, ssm_w_out_l1, norm_l2, s5_w_in_l2, lam_re_l2, lam_im_l2, log_step_l2, b_re_l2, b_im_l2, c_re_l2, c_im_l2, d_l2, glu_w_l2, glu_b_l2, s5_w_out_l2, norm_l3, attn_w_in_l3, lam_q1_l3, lam_k1_l3, lam_q2_l3, lam_k2_l3, subln_l3, attn_w_out_l3, final_norm):
    return _forward(_CFG, x_prompt, x_sample, cache_k_l0, cache_v_l0, state_conv_l1, state_ssm_l1, state_s5_re_l2, state_s5_im_l2, cache_k_l3, cache_v_l3, norm_l0, attn_w_in_l0, lam_q1_l0, lam_k1_l0, lam_q2_l0, lam_k2_l0, subln_l0, attn_w_out_l0, norm_l1, ssm_w_in_l1, conv_w_l1, conv_b_l1, dt_bias_l1, a_log_l1, d_l1, gnorm_l1, ssm_w_out_l1, norm_l2, s5_w_in_l2, lam_re_l2, lam_im_l2, log_step_l2, b_re_l2, b_im_l2, c_re_l2, c_im_l2, d_l2, glu_w_l2, glu_b_l2, s5_w_out_l2, norm_l3, attn_w_in_l3, lam_q1_l3, lam_k1_l3, lam_q2_l3, lam_k2_l3, subln_l3, attn_w_out_l3, final_norm)
```

```python
import dataclasses
import functools
import math

import jax
import jax.numpy as jnp
from jax import lax
from jax.experimental import pallas as pl
from jax.experimental.pallas import tpu as pltpu

F32 = jnp.float32
BF16 = jnp.bfloat16

LANES = 128
SUBLANES = 8
VMEM_LIMIT_BYTES = 48 * 1024 * 1024


@dataclasses.dataclass(frozen=True)
class Cfg:
    d_model: int = 4096
    seq: int = 16384
    dec_batch: int = 16
    dec_seq: int = 16
    past_len: int = 1024
    chunk: int = 64
    rms_eps: float = 1e-6
    rope_theta: float = 10000.0
    a_heads: int = 16
    a_head_dim: int = 128
    b_head_dim: int = 64
    b_groups: int = 8
    b_state: int = 128
    b_conv: int = 4
    c_group: int = 16
    c_state: int = 64
    attn_tq: int = 512
    attn_tk: int = 512
    ssd_chunk: int = 128
    s5_seg: int = 256
    s5_tt: int = 128
    mm_tm: int = 1024
    mm_tn: int = 512

    @property
    def a_vdim(self):
        return 2 * self.a_head_dim

    @property
    def a_width(self):
        return self.a_heads * self.a_vdim

    @property
    def b_width(self):
        return 2 * self.d_model

    @property
    def b_heads(self):
        return self.b_width // self.b_head_dim

    @property
    def b_conv_dim(self):
        return self.b_width + 2 * self.b_groups * self.b_state

    @property
    def c_width(self):
        return self.d_model

    @property
    def c_groups(self):
        return self.c_width // self.c_group


def _cp(sem):
    return pltpu.CompilerParams(dimension_semantics=sem, vmem_limit_bytes=VMEM_LIMIT_BYTES)


def _tile(n, pref, mult):
    if n <= pref:
        return n
    t = (pref // mult) * mult
    while t > mult and n % t:
        t -= mult
    assert n % t == 0, (n, pref, mult)
    return t


def _silu(x):
    return x / (1.0 + jnp.exp(-x))


def _rmsnorm_kernel(x_ref, w_ref, o_ref, *, eps):
    x = x_ref[...].astype(F32)
    ms = jnp.mean(x * x, axis=-1, keepdims=True)
    o_ref[...] = (x * lax.rsqrt(ms + eps) * w_ref[...]).astype(o_ref.dtype)


def _rmsnorm(x, w, out_dtype, eps, name):
    R, D = x.shape
    tr = _tile(R, 256, 8)
    return pl.pallas_call(
        functools.partial(_rmsnorm_kernel, eps=eps),
        grid=(R // tr,),
        in_specs=[pl.BlockSpec((tr, D), lambda i: (i, 0)), pl.BlockSpec((1, D), lambda i: (0, 0))],
        out_specs=pl.BlockSpec((tr, D), lambda i: (i, 0)),
        out_shape=jax.ShapeDtypeStruct((R, D), out_dtype),
        compiler_params=_cp(("parallel",)),
        name=name,
    )(x, w.reshape(1, D).astype(F32))


def _gated_rmsnorm_kernel(y_ref, z_ref, w_ref, o_ref, *, eps):
    y = y_ref[...].astype(F32) * _silu(z_ref[...].astype(F32))
    ms = jnp.mean(y * y, axis=-1, keepdims=True)
    o_ref[...] = (y * lax.rsqrt(ms + eps) * w_ref[...]).astype(o_ref.dtype)


def _gated_rmsnorm(y, z, w, eps, name):
    R, D = y.shape
    tr = _tile(R, 128, 8)
    return pl.pallas_call(
        functools.partial(_gated_rmsnorm_kernel, eps=eps),
        grid=(R // tr,),
        in_specs=[pl.BlockSpec((tr, D), lambda i: (i, 0)), pl.BlockSpec((tr, D), lambda i: (i, 0)),
                  pl.BlockSpec((1, D), lambda i: (0, 0))],
        out_specs=pl.BlockSpec((tr, D), lambda i: (i, 0)),
        out_shape=jax.ShapeDtypeStruct((R, D), BF16),
        compiler_params=_cp(("parallel",)),
        name=name,
    )(y, z, w.reshape(1, D).astype(F32))


def _mm_kernel(*refs, n_extra, epi):
    a_ref, w_ref = refs[0], refs[1]
    extras = refs[2:2 + n_extra]
    outs = refs[2 + n_extra:]
    acc = jnp.dot(a_ref[...], w_ref[...], preferred_element_type=F32)
    res = epi(acc, *[e[...] for e in extras])
    for o, r in zip(outs, res):
        if isinstance(r, (list, tuple)):
            for idx, piece in enumerate(r):
                o[idx] = piece.astype(o.dtype)
        else:
            o[...] = r.astype(o.dtype)


def _matmul(a, w, col0, ncols, out_dtypes, epi, extras, tm, tn, name):
    R, K = a.shape
    tm = _tile(R, tm, 8)
    tn = _tile(ncols, tn, LANES)
    assert col0 % tn == 0
    cb0 = col0 // tn
    in_specs = [pl.BlockSpec((tm, K), lambda i, j: (i, 0)),
                pl.BlockSpec((K, tn), lambda i, j: (0, cb0 + j))]
    args = [a, w]
    for arr, kind in extras:
        if kind == "row":
            in_specs.append(pl.BlockSpec((tm, arr.shape[1]), lambda i, j: (i, 0)))
        elif kind == "tile":
            in_specs.append(pl.BlockSpec((tm, tn), lambda i, j: (i, j)))
        elif kind == "col":
            in_specs.append(pl.BlockSpec((1, tn), lambda i, j: (0, j)))
        else:
            raise ValueError(kind)
        args.append(arr)
    out_specs, out_shape = [], []
    for dt in out_dtypes:
        if isinstance(dt, tuple):
            out_shape.append(dt[0])
            out_specs.append(dt[1](tm, tn))
        else:
            out_shape.append(jax.ShapeDtypeStruct((R, ncols), dt))
            out_specs.append(pl.BlockSpec((tm, tn), lambda i, j: (i, j)))
    return pl.pallas_call(
        functools.partial(_mm_kernel, n_extra=len(extras), epi=epi),
        grid=(R // tm, ncols // tn),
        in_specs=in_specs, out_specs=out_specs, out_shape=out_shape,
        compiler_params=_cp(("parallel", "arbitrary")),
        name=name,
    )(*args)


def _epi_plain(acc):
    return (acc,)


def _epi_two(acc):
    return (acc, acc)


def _rope_tile(acc, cos, sin):
    parts = []
    for c in range(acc.shape[1] // LANES):
        x = acc[:, c * LANES:(c + 1) * LANES]
        parts.append(x * cos + pltpu.roll(x, LANES // 2, axis=1) * sin)
    return parts[0] if len(parts) == 1 else jnp.concatenate(parts, axis=1)


def _epi_v_transposed(acc, *, tk):
    return (acc, [acc[r * tk:(r + 1) * tk, :].T for r in range(acc.shape[0] // tk)])


def _epi_rope_q(acc, cos, sin, *, scale):
    return (_rope_tile(acc, cos, sin) * scale,)


def _epi_rope_k(acc, cos, sin):
    r = _rope_tile(acc, cos, sin)
    return (r, r)


def _epi_residual(acc, x):
    return (x + acc,)


def _epi_softplus_bias(acc, b):
    v = acc + b
    return (jnp.maximum(v, 0.0) + jnp.log(1.0 + jnp.exp(-jnp.abs(v))),)


def _epi_glu(acc, b, yg, g):
    ygf = yg.astype(F32)
    sig = 1.0 / (1.0 + jnp.exp(-(acc + b)))
    return (ygf * sig * _silu(g.astype(F32)),)


def _rope_table_kernel(inv_ref, cos_ref, sin_ref, *, tr, pos0, period):
    i = pl.program_id(0)
    row = lax.broadcasted_iota(jnp.int32, (tr, LANES), 0) + i * tr
    pos = (pos0 + row % period).astype(F32)
    ang = pos * inv_ref[...]
    lane = lax.broadcasted_iota(jnp.int32, (tr, LANES), 1)
    cos_ref[...] = jnp.cos(ang)
    sin_ref[...] = jnp.where(lane < LANES // 2, -jnp.sin(ang), jnp.sin(ang))


def _rope_tables(cfg, rows, pos0, period, name):
    dh = cfg.a_head_dim
    inv = 1.0 / (cfg.rope_theta ** (jnp.arange(0, dh, 2, dtype=F32) / dh))
    inv2 = jnp.concatenate([inv, inv]).reshape(1, dh)
    tr = _tile(rows, 512, 8)
    return pl.pallas_call(
        functools.partial(_rope_table_kernel, tr=tr, pos0=pos0, period=period),
        grid=(rows // tr,),
        in_specs=[pl.BlockSpec((1, dh), lambda i: (0, 0))],
        out_specs=[pl.BlockSpec((tr, dh), lambda i: (i, 0))] * 2,
        out_shape=[jax.ShapeDtypeStruct((rows, dh), F32)] * 2,
        compiler_params=_cp(("parallel",)),
        name=name,
    )(inv2)


def _lambda(lamv, lam_init):
    s1 = jnp.sum(lamv[0:1, :] * lamv[1:2, :], axis=-1, keepdims=True)
    s2 = jnp.sum(lamv[2:3, :] * lamv[3:4, :], axis=-1, keepdims=True)
    return jnp.exp(s1) - jnp.exp(s2) + lam_init


def _attn_finish(o, g, sw, lam_init, eps):
    ms = jnp.mean(o * o, axis=-1, keepdims=True)
    on = (o * lax.rsqrt(ms + eps) * sw) * (1.0 - lam_init)
    return (on * _silu(g.astype(F32))).astype(BF16)


def _attn_prompt_kernel(lamv_ref, q_ref, k_ref, vt_ref, g_ref, sw_ref, o_ref, m_sc, l_sc, acc_sc, sa_sc, sb_sc,
                        *, t, dh, chunk, lam_init, eps):
    qi = pl.program_id(1)
    q = q_ref[...]
    m_sc[...] = jnp.full(m_sc.shape, -jnp.inf, F32)
    l_sc[...] = jnp.zeros(l_sc.shape, F32)
    acc_sc[...] = jnp.zeros(acc_sc.shape, F32)

    def scores(kb, s_sc):
        kblk = k_ref[pl.ds(pl.multiple_of(kb * t, t), t), :]
        for mp in range(2):
            s_sc[mp] = lax.dot_general(kblk[:, mp * dh:(mp + 1) * dh], q[:, mp * dh:(mp + 1) * dh],
                                       (((1,), (1,)), ((), ())), preferred_element_type=F32)

    def consume(kb, s_sc, masked):
        vt = vt_ref[kb]
        if masked:
            keys = lax.broadcasted_iota(jnp.int32, (t, t), 0)
            qs = lax.broadcasted_iota(jnp.int32, (t, t), 1)
            visible = (keys // chunk) <= (qs // chunk)
        for mp in range(2):
            st = s_sc[mp]
            if masked:
                st = jnp.where(visible, st, -jnp.inf)
            m_prev = m_sc[mp]
            m_new = jnp.maximum(m_prev, jnp.max(st, axis=0, keepdims=True))
            alpha = jnp.exp2(m_prev - m_new)
            p = jnp.exp2(st - m_new)
            l_sc[mp] = alpha * l_sc[mp] + jnp.sum(p, axis=0, keepdims=True)
            acc_sc[mp] = alpha * acc_sc[mp] + jnp.dot(vt, p.astype(BF16), preferred_element_type=F32)
            m_sc[mp] = m_new

    nfull = qi
    scores(0, sa_sc)

    def body(i, c):
        kb = 2 * i
        consume(kb, sa_sc, False)
        scores(kb + 1, sb_sc)
        consume(kb + 1, sb_sc, False)
        scores(kb + 2, sa_sc)
        return c

    lax.fori_loop(0, nfull // 2, body, 0)

    @pl.when(nfull % 2 == 0)
    def _():
        consume(nfull, sa_sc, True)

    @pl.when(nfull % 2 == 1)
    def _():
        consume(nfull - 1, sa_sc, False)
        scores(nfull, sb_sc)
        consume(nfull, sb_sc, True)

    lam = _lambda(lamv_ref[...], lam_init)
    ot = acc_sc[0] / l_sc[0] - lam * (acc_sc[1] / l_sc[1])
    o_ref[...] = _attn_finish(ot.T, g_ref[...], sw_ref[...], lam_init, eps)


def _attn_prompt(cfg, q, k, vt3, g, lamv, sw, lam_init, name):
    L = q.shape[0]
    H, dv, dh = cfg.a_heads, cfg.a_vdim, cfg.a_head_dim
    tq = tk = vt3.shape[2]
    assert vt3.shape[0] * tk == L and tk % cfg.chunk == 0
    kern = functools.partial(_attn_prompt_kernel, t=tk, dh=dh, chunk=cfg.chunk,
                             lam_init=lam_init, eps=cfg.rms_eps)
    return pl.pallas_call(
        kern,
        grid=(H, L // tq),
        in_specs=[pl.BlockSpec((4, dh), lambda h, i: (0, 0)),
                  pl.BlockSpec((tq, dv), lambda h, i: (i, h)),
                  pl.BlockSpec((L, dv), lambda h, i: (0, h)),
                  pl.BlockSpec((L // tk, dv, tk), lambda h, i: (0, h, 0)),
                  pl.BlockSpec((tq, dv), lambda h, i: (i, h)),
                  pl.BlockSpec((1, dv), lambda h, i: (0, 0))],
        out_specs=pl.BlockSpec((tq, dv), lambda h, i: (i, h)),
        out_shape=jax.ShapeDtypeStruct((L, H * dv), BF16),
        scratch_shapes=[pltpu.VMEM((2, 1, tq), F32), pltpu.VMEM((2, 1, tq), F32),
                        pltpu.VMEM((2, dv, tq), F32), pltpu.VMEM((2, tk, tq), F32), pltpu.VMEM((2, tk, tq), F32)],
        compiler_params=_cp(("parallel", "arbitrary")),
        name=name,
    )(lamv, q, k, vt3, g, sw)


def _attn_sample_kernel(lamv_ref, q_ref, kn_ref, vn_ref, kc_ref, vc_ref, g_ref, sw_ref, o_ref,
                        *, P, Ls, dh, chunk, lam_init, eps):
    q = q_ref[...]
    kn = kn_ref[...]
    vn = vn_ref[...]
    kc = kc_ref[0].astype(BF16)
    vc = vc_ref[0].astype(BF16)
    qpos_p = P + lax.broadcasted_iota(jnp.int32, (Ls, P), 0)
    kpos_p = lax.broadcasted_iota(jnp.int32, (Ls, P), 1)
    vis_p = (kpos_p // chunk) <= (qpos_p // chunk)
    qpos_n = P + lax.broadcasted_iota(jnp.int32, (Ls, Ls), 0)
    kpos_n = P + lax.broadcasted_iota(jnp.int32, (Ls, Ls), 1)
    vis_n = (kpos_n // chunk) <= (qpos_n // chunk)
    outs = []
    dn = (((1,), (1,)), ((), ()))
    for mp in range(2):
        qm = q[:, mp * dh:(mp + 1) * dh]
        sp = lax.dot_general(qm, kc[:, mp * dh:(mp + 1) * dh], dn, preferred_element_type=F32)
        sn = lax.dot_general(qm, kn[:, mp * dh:(mp + 1) * dh], dn, preferred_element_type=F32)
        sp = jnp.where(vis_p, sp, -jnp.inf)
        sn = jnp.where(vis_n, sn, -jnp.inf)
        m = jnp.maximum(jnp.max(sp, axis=-1, keepdims=True), jnp.max(sn, axis=-1, keepdims=True))
        pp = jnp.exp2(sp - m)
        pn = jnp.exp2(sn - m)
        l = jnp.sum(pp, axis=-1, keepdims=True) + jnp.sum(pn, axis=-1, keepdims=True)
        acc = (jnp.dot(pp.astype(BF16), vc, preferred_element_type=F32)
               + jnp.dot(pn.astype(BF16), vn, preferred_element_type=F32))
        outs.append(acc / l)
    lam = _lambda(lamv_ref[...], lam_init)
    o_ref[...] = _attn_finish(outs[0] - lam * outs[1], g_ref[...], sw_ref[...], lam_init, eps)


def _attn_sample(cfg, q, k, v, g, kc, vc, lamv, sw, lam_init, name):
    B, Ls, P = cfg.dec_batch, cfg.dec_seq, cfg.past_len
    H, dv, dh = cfg.a_heads, cfg.a_vdim, cfg.a_head_dim
    kern = functools.partial(_attn_sample_kernel, P=P, Ls=Ls, dh=dh, chunk=cfg.chunk,
                             lam_init=lam_init, eps=cfg.rms_eps)
    row = pl.BlockSpec((Ls, dv), lambda b, h: (b, h))
    cache = pl.BlockSpec((1, P, dv), lambda b, h: (b, 0, h))
    return pl.pallas_call(
        kern,
        grid=(B, H),
        in_specs=[pl.BlockSpec((4, dh), lambda b, h: (0, 0)), row, row, row, cache, cache, row,
                  pl.BlockSpec((1, dv), lambda b, h: (0, 0))],
        out_specs=row,
        out_shape=jax.ShapeDtypeStruct((B * Ls, H * dv), BF16),
        compiler_params=_cp(("parallel", "parallel")),
        name=name,
    )(lamv, q, k, v, kc.reshape(B, P, H * dv), vc.reshape(B, P, H * dv), g, sw)


def _attn_layer(cfg, li, xp, xs, cache_k, cache_v, nw, w_in, lq1, lk1, lq2, lk2, subln, w_out, ropes):
    lam_init = 0.8 - 0.6 * math.exp(-0.3 * li)
    W = cfg.a_width
    w_in_b = w_in.astype(BF16)
    w_out_b = w_out.astype(BF16)
    lamv = jnp.stack([lq1, lk1, lq2, lk2]).astype(F32)
    sw = subln.reshape(1, cfg.a_vdim).astype(F32)
    scale = cfg.a_head_dim ** -0.5 * math.log2(math.e)
    res = []
    for tag, x, (cos, sin) in (("p", xp, ropes[0]), ("s", xs, ropes[1])):
        nm = f"l{li}{tag}"
        h = _rmsnorm(x, nw, BF16, cfg.rms_eps, nm + "_norm")
        tm, tn = cfg.mm_tm, cfg.mm_tn
        rope_ex = [(cos, "row"), (sin, "row")]
        (q,) = _matmul(h, w_in_b, 0, W, [BF16], functools.partial(_epi_rope_q, scale=scale), rope_ex,
                       tm, tn, nm + "_q")
        k32, k16 = _matmul(h, w_in_b, W, W, [F32, BF16], _epi_rope_k, rope_ex, tm, tn, nm + "_k")
        (g,) = _matmul(h, w_in_b, 3 * W, W, [BF16], _epi_plain, [], tm, tn, nm + "_g")
        if tag == "p":
            L = x.shape[0]
            tk = _tile(L, cfg.attn_tk, LANES)
            vt_out = (jax.ShapeDtypeStruct((L // tk, W, tk), BF16),
                      lambda bm, bn: pl.BlockSpec((bm // tk, bn, tk), lambda i, j: (i, j, 0)))
            v32, vt3 = _matmul(h, w_in_b, 2 * W, W, [F32, vt_out], functools.partial(_epi_v_transposed, tk=tk),
                               [], tm, tn, nm + "_v")
            y = _attn_prompt(cfg, q, k16, vt3, g, lamv, sw, lam_init, nm + "_attn")
        else:
            v32, v16 = _matmul(h, w_in_b, 2 * W, W, [F32, BF16], _epi_two, [], tm, tn, nm + "_v")
            y = _attn_sample(cfg, q, k16, v16, g, cache_k, cache_v, lamv, sw, lam_init, nm + "_attn")
        (xn,) = _matmul(y, w_out_b, 0, cfg.d_model, [F32], _epi_residual, [(x, "tile")], tm, tn, nm + "_out")
        res.append((xn, k32, v32))
    return res


def _conv_kernel(x_ref, prev_ref, w_ref, b_ref, o_ref, carry_sc, *, tr, taps):
    r = pl.program_id(2)

    @pl.when(r == 0)
    def _():
        carry_sc[...] = prev_ref[0]

    x = x_ref[...].astype(F32)
    w = w_ref[...]
    b = b_ref[...]
    acc = b + x * w[taps - 1:taps, :]
    for j in range(1, taps):
        acc = acc + pltpu.roll(x, j, axis=0) * w[taps - 1 - j:taps - j, :]
    o_ref[...] = _silu(acc).astype(o_ref.dtype)
    x8 = x[0:SUBLANES, :]
    pr = carry_sc[...]
    row8 = lax.broadcasted_iota(jnp.int32, x8.shape, 0)
    acc8 = b + x8 * w[taps - 1:taps, :]
    for j in range(1, taps):
        sh = jnp.where(row8 < j, pltpu.roll(pr, j, axis=0), pltpu.roll(x8, j, axis=0))
        acc8 = acc8 + sh * w[taps - 1 - j:taps - j, :]
    o_ref[0:SUBLANES, :] = _silu(acc8).astype(o_ref.dtype)
    carry_sc[...] = x[tr - SUBLANES:tr, :]


def _conv_silu(xbc, prev8, w_t, b, nseq, name):
    R, C = xbc.shape
    taps = w_t.shape[0]
    Lseq = R // nseq
    tr = _tile(Lseq, 512, 16)
    tc = _tile(C, 512, LANES)
    nr = Lseq // tr
    return pl.pallas_call(
        functools.partial(_conv_kernel, tr=tr, taps=taps),
        grid=(C // tc, nseq, nr),
        in_specs=[pl.BlockSpec((tr, tc), lambda c, s, r: (s * nr + r, c)),
                  pl.BlockSpec((1, SUBLANES, tc), lambda c, s, r: (s, 0, c)),
                  pl.BlockSpec((taps, tc), lambda c, s, r: (0, c)),
                  pl.BlockSpec((1, tc), lambda c, s, r: (0, c))],
        out_specs=pl.BlockSpec((tr, tc), lambda c, s, r: (s * nr + r, c)),
        out_shape=jax.ShapeDtypeStruct((R, C), BF16),
        scratch_shapes=[pltpu.VMEM((SUBLANES, tc), F32)],
        compiler_params=_cp(("parallel", "arbitrary", "arbitrary")),
        name=name,
    )(xbc, prev8, w_t, b)


def _split_bf16(v):
    hi = v.astype(BF16)
    lo = (v - hi.astype(F32)).astype(BF16)
    return hi, lo


def _expand(v, e):
    hi, lo = _split_bf16(v)
    return (jnp.dot(hi, e, preferred_element_type=F32) + jnp.dot(lo, e, preferred_element_type=F32))


def _ssd_kernel(*refs, T, gw, hg, hd, N, has_init):
    if has_init:
        x_ref, b_ref, c_ref, dt_ref, alog_ref, d_ref, e_ref, init_ref, y_ref, st_ref, s_sc = refs
    else:
        x_ref, b_ref, c_ref, dt_ref, alog_ref, d_ref, e_ref, y_ref, st_ref, s_sc = refs
    ci = pl.program_id(2)

    @pl.when(ci == 0)
    def _():
        if has_init:
            s_sc[...] = init_ref[0].reshape(gw, N).T
        else:
            s_sc[...] = jnp.zeros(s_sc.shape, F32)

    dt = dt_ref[...]
    a_neg = -jnp.exp(alog_ref[...])
    a = dt * a_neg
    row = lax.broadcasted_iota(jnp.int32, a.shape, 0)
    d = 1
    while d < T:
        a = a + jnp.where(row >= d, pltpu.roll(a, d, axis=0), 0.0)
        d *= 2
    a_last = a[T - 1:T, :]
    e = e_ref[...]
    dt_x = _expand(dt, e)
    expa_x = _expand(jnp.exp(a), e)
    decay_x = _expand(jnp.exp(a_last - a), e)
    x = x_ref[...].astype(F32)
    xdt = x * dt_x
    xdt_b = xdt.astype(BF16)
    bm = b_ref[...]
    cm = c_ref[...]
    cb = lax.dot_general(cm, bm, (((1,), (1,)), ((), ())), preferred_element_type=F32)
    s0 = s_sc[...]
    y_off = jnp.dot(cm, s0.astype(BF16), preferred_element_type=F32) * expa_x
    a_t = a.T
    tri = (lax.broadcasted_iota(jnp.int32, (T, T), 0) >= lax.broadcasted_iota(jnp.int32, (T, T), 1))
    lane = lax.broadcasted_iota(jnp.int32, (T, LANES), 1)
    per_pair = LANES // hd
    pieces = []
    for pi in range(gw // LANES):
        xp = xdt_b[:, pi * LANES:(pi + 1) * LANES]
        piece = None
        for e_i in range(per_pair):
            hl = pi * per_pair + e_i
            seg = a[:, hl:hl + 1] - a_t[hl:hl + 1, :]
            m = (cb * jnp.where(tri, jnp.exp(seg), 0.0)).astype(BF16)
            r = jnp.dot(m, xp, preferred_element_type=F32)
            if piece is None:
                piece = r
            else:
                piece = jnp.where(lane < e_i * hd, piece, r)
        pieces.append(piece)
    y_diag = pieces[0] if len(pieces) == 1 else jnp.concatenate(pieces, axis=1)
    y_ref[...] = (y_diag + y_off + d_ref[...] * x).astype(y_ref.dtype)
    xw = (xdt * decay_x).astype(BF16)
    upd = lax.dot_general(bm, xw, (((0,), (0,)), ((), ())), preferred_element_type=F32)
    s_new = s0 * expa_x[T - 1:T, :] + upd
    s_sc[...] = s_new

    @pl.when(ci == pl.num_programs(2) - 1)
    def _():
        st_ref[0] = s_new.T.reshape(hg, hd, N)


def _ssd(cfg, xbc_act, dtg, alog_g, d_exp, e_mat, init, nseq, T, name):
    R = xbc_act.shape[0]
    G, N, hd = cfg.b_groups, cfg.b_state, cfg.b_head_dim
    hg = cfg.b_heads // G
    gw = hg * hd
    Lseq = R // nseq
    nc = Lseq // T
    bw = cfg.b_width
    boff = bw // N
    has_init = init is not None
    kern = functools.partial(_ssd_kernel, T=T, gw=gw, hg=hg, hd=hd, N=N, has_init=has_init)
    in_specs = [pl.BlockSpec((T, gw), lambda s, g, c: (s * nc + c, g)),
                pl.BlockSpec((T, N), lambda s, g, c: (s * nc + c, boff + g)),
                pl.BlockSpec((T, N), lambda s, g, c: (s * nc + c, boff + G + g)),
                pl.BlockSpec((T, LANES), lambda s, g, c: (s * nc + c, g)),
                pl.BlockSpec((1, LANES), lambda s, g, c: (0, g)),
                pl.BlockSpec((1, gw), lambda s, g, c: (0, g)),
                pl.BlockSpec((LANES, gw), lambda s, g, c: (0, 0))]
    args = [xbc_act, xbc_act, xbc_act, dtg, alog_g, d_exp, e_mat]
    if has_init:
        in_specs.append(pl.BlockSpec((1, hg, hd, N), lambda s, g, c: (s, g, 0, 0)))
        args.append(init)
    return pl.pallas_call(
        kern,
        grid=(nseq, G, nc),
        in_specs=in_specs,
        out_specs=[pl.BlockSpec((T, gw), lambda s, g, c: (s * nc + c, g)),
                   pl.BlockSpec((1, hg, hd, N), lambda s, g, c: (s, g, 0, 0))],
        out_shape=[jax.ShapeDtypeStruct((R, bw), BF16),
                   jax.ShapeDtypeStruct((nseq, cfg.b_heads, hd, N), F32)],
        scratch_shapes=[pltpu.VMEM((N, gw), F32)],
        compiler_params=_cp(("parallel", "parallel", "arbitrary")),
        name=name,
    )(*args)


def _ssd_layer(cfg, li, xp, xs, conv_prev, ssm_prev, nw, w_in, conv_w, conv_b, dt_bias, a_log, d_skip,
               gnorm, w_out):
    bw, cd, H, G = cfg.b_width, cfg.b_conv_dim, cfg.b_heads, cfg.b_groups
    hg = H // G
    hd = cfg.b_head_dim
    gw = hg * hd
    w_in_b = w_in.astype(BF16)
    w_out_b = w_out.astype(BF16)

    def per_group(v):
        lead = v.shape[:-1]
        v = v.reshape(lead + (G, hg))
        v = jnp.pad(v, [(0, 0)] * len(lead) + [(0, 0), (0, LANES - hg)])
        return v.reshape(lead + (G * LANES,))

    w_dt_g = per_group(w_in[:, bw + cd:]).astype(BF16)
    dtb_g = per_group(dt_bias.astype(F32)).reshape(1, G * LANES)
    alog_g = per_group(a_log.astype(F32)).reshape(1, G * LANES)
    d_exp = jnp.repeat(d_skip.astype(F32), hd).reshape(1, bw)
    e_mat = (jnp.arange(LANES)[:, None] == (jnp.arange(gw)[None, :] // hd)).astype(BF16)
    w_t = conv_w.astype(F32).T
    cb2 = conv_b.astype(F32).reshape(1, cd)
    T = cfg.ssd_chunk
    B, Ls = cfg.dec_batch, cfg.dec_seq
    res = []
    for tag, x in (("p", xp), ("s", xs)):
        nm = f"l{li}{tag}"
        tm, tn = cfg.mm_tm, cfg.mm_tn
        h = _rmsnorm(x, nw, BF16, cfg.rms_eps, nm + "_norm")
        (z,) = _matmul(h, w_in_b, 0, bw, [BF16], _epi_plain, [], tm, tn, nm + "_z")
        (xbc,) = _matmul(h, w_in_b, bw, cd, [BF16], _epi_plain, [], tm, tn, nm + "_xbc")
        (dtg,) = _matmul(h, w_dt_g, 0, G * LANES, [F32], _epi_softplus_bias, [(dtb_g, "col")], tm, tn,
                         nm + "_dt")
        if tag == "p":
            prev8 = jnp.zeros((1, SUBLANES, cd), F32)
            act = _conv_silu(xbc, prev8, w_t, cb2, 1, nm + "_conv")
            conv_new = xbc[x.shape[0] - (cfg.b_conv - 1):].astype(F32)[None]
            y, st = _ssd(cfg, act, dtg, alog_g, d_exp, e_mat, None, 1, T, nm + "_ssd")
        else:
            prev8 = jnp.pad(conv_prev.astype(F32), ((0, 0), (SUBLANES - (cfg.b_conv - 1), 0), (0, 0)))
            act = _conv_silu(xbc, prev8, w_t, cb2, B, nm + "_conv")
            conv_new = xbc.reshape(B, Ls, cd)[:, Ls - (cfg.b_conv - 1):].astype(F32)
            act_p = jnp.pad(act.reshape(B, Ls, cd), ((0, 0), (0, T - Ls), (0, 0))).reshape(B * T, cd)
            dt_p = jnp.pad(dtg.reshape(B, Ls, G * LANES), ((0, 0), (0, T - Ls), (0, 0))).reshape(B * T, G * LANES)
            y_p, st = _ssd(cfg, act_p, dt_p, alog_g, d_exp, e_mat, ssm_prev.astype(F32), B, T, nm + "_ssd")
            y = y_p.reshape(B, T, bw)[:, :Ls].reshape(B * Ls, bw)
        yn = _gated_rmsnorm(y, z, gnorm, cfg.rms_eps, nm + "_gnorm")
        (xn,) = _matmul(yn, w_out_b, 0, cfg.d_model, [F32], _epi_residual, [(x, "tile")], tm // 2, tn,
                        nm + "_out")
        res.append((xn, conv_new, st))
    return res


def _s5_disc_kernel(lr_ref, li_ref, ls_ref, br_ref, bi_ref, ar_ref, ai_ref, bbr_ref, bbi_ref):
    lr = lr_ref[...]
    li = li_ref[...]
    step = jnp.exp(ls_ref[...])
    mag = jnp.exp(lr * step)
    ar = mag * jnp.cos(li * step)
    ai = mag * jnp.sin(li * step)
    den = lr * lr + li * li
    nr = ar - 1.0
    ni = ai
    cr = ((nr * lr + ni * li) / den)[:, None, :]
    ci = ((ni * lr - nr * li) / den)[:, None, :]
    br = br_ref[...]
    bi = bi_ref[...]
    ar_ref[...] = ar
    ai_ref[...] = ai
    bbr_ref[...] = cr * br - ci * bi
    bbi_ref[...] = cr * bi + ci * br


def _s5_discretize(cfg, lam_re, lam_im, log_step, b_re, b_im, name):
    G, P, K = cfg.c_groups, cfg.c_state, cfg.c_group
    bt_r = jnp.transpose(b_re.astype(F32), (0, 2, 1))
    bt_i = jnp.transpose(b_im.astype(F32), (0, 2, 1))
    tg = _tile(G, 64, 8)
    s2 = pl.BlockSpec((tg, P), lambda i: (i, 0))
    s3 = pl.BlockSpec((tg, K, P), lambda i: (i, 0, 0))
    return pl.pallas_call(
        _s5_disc_kernel,
        grid=(G // tg,),
        in_specs=[s2, s2, pl.BlockSpec((tg, 1), lambda i: (i, 0)), s3, s3],
        out_specs=[s2, s2, s3, s3],
        out_shape=[jax.ShapeDtypeStruct((G, P), F32)] * 2 + [jax.ShapeDtypeStruct((G, K, P), F32)] * 2,
        compiler_params=_cp(("parallel",)),
        name=name,
    )(lam_re.astype(F32), lam_im.astype(F32), log_step.astype(F32).reshape(G, 1), bt_r, bt_i)


def _gelu_tanh(v):
    return 0.5 * v * (1.0 + jnp.tanh(math.sqrt(2.0 / math.pi) * (v + 0.044715 * (v * v * v))))


def _s5_scan_kernel(*refs, tT, pitch, sw, emit_y):
    if emit_y:
        (u_ref, bbd_ref, cbd_ref, ar_ref, ai_ref, d_ref, hr0_ref, hi0_ref,
         yg_ref, hrT_ref, hiT_ref, bu_sc, hs_sc, hr_sc, hi_sc) = refs
    else:
        (u_ref, bbd_ref, ar_ref, ai_ref, hr0_ref, hi0_ref,
         hrT_ref, hiT_ref, bu_sc, hr_sc, hi_sc) = refs
    tb = pl.program_id(2)

    @pl.when(tb == 0)
    def _():
        hr_sc[...] = hr0_ref[...]
        hi_sc[...] = hi0_ref[...]

    ub = u_ref[...].reshape(SUBLANES * tT, LANES)
    bu = jnp.dot(ub, bbd_ref[0], preferred_element_type=F32)
    nl = sw // LANES
    for l in range(2 * nl):
        for sq in range(SUBLANES):
            bu_sc[l, sq * pitch:sq * pitch + tT, :] = bu[sq * tT:(sq + 1) * tT, l * LANES:(l + 1) * LANES]
    ar = jnp.broadcast_to(ar_ref[...], (SUBLANES, sw))
    ai = jnp.broadcast_to(ai_ref[...], (SUBLANES, sw))

    def body(t, carry):
        hr, hi = carry
        rows = pl.ds(t, SUBLANES, stride=pitch)
        bur = jnp.concatenate([bu_sc[l, rows, :] for l in range(nl)], axis=1)
        bui = jnp.concatenate([bu_sc[nl + l, rows, :] for l in range(nl)], axis=1)
        nr = ar * hr - ai * hi + bur
        ni = ar * hi + ai * hr + bui
        if emit_y:
            for l in range(nl):
                hs_sc[l, rows, :] = nr[:, l * LANES:(l + 1) * LANES]
                hs_sc[nl + l, rows, :] = ni[:, l * LANES:(l + 1) * LANES]
        return nr, ni

    hr, hi = lax.fori_loop(0, tT, body, (hr_sc[...], hi_sc[...]), unroll=4)
    hr_sc[...] = hr
    hi_sc[...] = hi
    if emit_y:
        hs = jnp.concatenate(
            [jnp.concatenate([hs_sc[l, sq * pitch:sq * pitch + tT, :] for sq in range(SUBLANES)], axis=0)
             for l in range(2 * nl)], axis=1).astype(BF16)
        y = jnp.dot(hs, cbd_ref[0], preferred_element_type=F32)
        v = y + d_ref[...] * ub.astype(F32)
        yg_ref[...] = _gelu_tanh(v).reshape(SUBLANES, tT, LANES).astype(yg_ref.dtype)

    @pl.when(tb == pl.num_programs(2) - 1)
    def _():
        hrT_ref[...] = hr
        hiT_ref[...] = hi


def _s5_scan(cfg, u3, bbd, cbd, ar, ai, d2, hr0, hi0, emit_y, name):
    NS, T, C = u3.shape
    nj = C // LANES
    sw = bbd.shape[2] // 2
    tT = _tile(T, cfg.s5_tt, 16)
    nsg = NS // SUBLANES
    pitch = tT + SUBLANES if (tT // SUBLANES) % 2 == 0 else tT
    kern = functools.partial(_s5_scan_kernel, tT=tT, pitch=pitch, sw=sw, emit_y=emit_y)
    u_spec = pl.BlockSpec((SUBLANES, tT, LANES), lambda j, s, t: (s, t, j))
    bbd_spec = pl.BlockSpec((1, LANES, 2 * sw), lambda j, s, t: (j, 0, 0))
    a_spec = pl.BlockSpec((1, sw), lambda j, s, t: (0, j))
    st_spec = pl.BlockSpec((SUBLANES, sw), lambda j, s, t: (s, j))
    st_shape = jax.ShapeDtypeStruct((NS, nj * sw), F32)
    scratch = [pltpu.VMEM((2 * sw // LANES, SUBLANES * pitch, LANES), F32)]
    if emit_y:
        in_specs = [u_spec, bbd_spec, pl.BlockSpec((1, 2 * sw, LANES), lambda j, s, t: (j, 0, 0)),
                    a_spec, a_spec, pl.BlockSpec((1, LANES), lambda j, s, t: (0, j)), st_spec, st_spec]
        args = [u3, bbd, cbd, ar, ai, d2, hr0, hi0]
        out_specs = [u_spec, st_spec, st_spec]
        out_shape = [jax.ShapeDtypeStruct((NS, T, C), BF16), st_shape, st_shape]
        scratch.append(pltpu.VMEM((2 * sw // LANES, SUBLANES * pitch, LANES), F32))
    else:
        in_specs = [u_spec, bbd_spec, a_spec, a_spec, st_spec, st_spec]
        args = [u3, bbd, ar, ai, hr0, hi0]
        out_specs = [st_spec, st_spec]
        out_shape = [st_shape, st_shape]
    scratch += [pltpu.VMEM((SUBLANES, sw), F32)] * 2
    return pl.pallas_call(
        kern,
        grid=(nj, nsg, T // tT),
        in_specs=in_specs, out_specs=out_specs, out_shape=out_shape,
        scratch_shapes=scratch,
        compiler_params=_cp(("parallel", "parallel", "arbitrary")),
        name=name,
    )(*args)


def _s5_combine_kernel(er_ref, ei_ref, ar_ref, ai_ref, h0r_ref, h0i_ref, hr_ref, hi_ref, *, nseg, nsq):
    pr = ar_ref[...]
    pi_ = ai_ref[...]
    for _ in range(nsq):
        pr, pi_ = pr * pr - pi_ * pi_, 2.0 * pr * pi_
    hr = h0r_ref[...]
    hi = h0i_ref[...]
    hr_ref[0:1, :] = hr
    hi_ref[0:1, :] = hi
    for i in range(1, nseg):
        er = er_ref[i - 1:i, :]
        ei = ei_ref[i - 1:i, :]
        hr, hi = pr * hr - pi_ * hi + er, pr * hi + pi_ * hr + ei
        hr_ref[i:i + 1, :] = hr
        hi_ref[i:i + 1, :] = hi


def _s5_combine(er, ei, ar, ai, h0r, h0i, seg_len, name):
    nseg, W = er.shape
    nsq = int(round(math.log2(seg_len)))
    assert 2 ** nsq == seg_len
    tw = _tile(W, 1024, LANES)
    full = pl.BlockSpec((nseg, tw), lambda i: (0, i))
    row = pl.BlockSpec((1, tw), lambda i: (0, i))
    return pl.pallas_call(
        functools.partial(_s5_combine_kernel, nseg=nseg, nsq=nsq),
        grid=(W // tw,),
        in_specs=[full, full, row, row, row, row],
        out_specs=[full, full],
        out_shape=[jax.ShapeDtypeStruct((nseg, W), F32)] * 2,
        compiler_params=_cp(("parallel",)),
        name=name,
    )(er, ei, ar, ai, h0r, h0i)


def _s5_layer(cfg, li, xp, xs, s_re, s_im, nw, w_in, lam_re, lam_im, log_step, b_re, b_im, c_re, c_im,
              d_skip, w_glu, b_glu, w_out):
    CW, G, P, K = cfg.c_width, cfg.c_groups, cfg.c_state, cfg.c_group
    gpb = LANES // K
    nj = CW // LANES
    sw = gpb * P
    w_in_b = w_in.astype(BF16)
    w_glu_b = w_glu.astype(BF16)
    w_out_b = w_out.astype(BF16)
    ar, ai, bbr, bbi = _s5_discretize(cfg, lam_re, lam_im, log_step, b_re, b_im, f"l{li}_disc")
    eye = jnp.eye(gpb, dtype=F32)
    bb = jnp.stack([bbr, bbi]).reshape(2, nj, gpb, K, P)
    bbd = jnp.einsum("rjgkp,gh->jgkrhp", bb, eye).reshape(nj, LANES, 2 * sw).astype(BF16)
    cc = jnp.stack([c_re.astype(F32), -c_im.astype(F32)]).reshape(2, nj, gpb, K, P)
    cbd = jnp.einsum("rjgkp,gh->jrgphk", cc, eye).reshape(nj, 2 * sw, LANES).astype(BF16)
    ar2 = ar.reshape(1, G * P)
    ai2 = ai.reshape(1, G * P)
    d2 = d_skip.astype(F32).reshape(1, CW)
    bg2 = b_glu.astype(F32).reshape(1, CW)
    B, Ls = cfg.dec_batch, cfg.dec_seq
    res = []
    for tag, x in (("p", xp), ("s", xs)):
        nm = f"l{li}{tag}"
        tm, tn = cfg.mm_tm, cfg.mm_tn
        h = _rmsnorm(x, nw, BF16, cfg.rms_eps, nm + "_norm")
        (u,) = _matmul(h, w_in_b, 0, CW, [BF16], _epi_plain, [], tm, tn, nm + "_u")
        (g,) = _matmul(h, w_in_b, CW, CW, [BF16], _epi_plain, [], tm, tn, nm + "_g")
        if tag == "p":
            L = x.shape[0]
            seg = _tile(L // SUBLANES, cfg.s5_seg, 16)
            nseg = L // seg
            u3 = u.reshape(nseg, seg, CW)
            zs = jnp.zeros((nseg, G * P), F32)
            er, ei = _s5_scan(cfg, u3, bbd, None, ar2, ai2, None, zs, zs, False, nm + "_scan_a")
            z1 = jnp.zeros((1, G * P), F32)
            h0r, h0i = _s5_combine(er, ei, ar2, ai2, z1, z1, seg, nm + "_comb")
            yg3, hrT, hiT = _s5_scan(cfg, u3, bbd, cbd, ar2, ai2, d2, h0r, h0i, True, nm + "_scan_b")
            yg = yg3.reshape(L, CW)
            st_r = hrT[nseg - 1:nseg].reshape(1, G, P)
            st_i = hiT[nseg - 1:nseg].reshape(1, G, P)
        else:
            u3 = u.reshape(B, Ls, CW)
            yg3, hrT, hiT = _s5_scan(cfg, u3, bbd, cbd, ar2, ai2, d2,
                                     s_re.astype(F32).reshape(B, G * P), s_im.astype(F32).reshape(B, G * P),
                                     True, nm + "_scan")
            yg = yg3.reshape(B * Ls, CW)
            st_r = hrT.reshape(B, G, P)
            st_i = hiT.reshape(B, G, P)
        (y2,) = _matmul(yg, w_glu_b, 0, CW, [BF16], _epi_glu, [(bg2, "col"), (yg, "tile"), (g, "tile")],
                        tm, tn, nm + "_glu")
        (xn,) = _matmul(y2, w_out_b, 0, cfg.d_model, [F32], _epi_residual, [(x, "tile")], tm, tn, nm + "_out")
        res.append((xn, st_r, st_i))
    return res


def _forward(cfg, x_prompt, x_sample,
             cache_k_l0, cache_v_l0, state_conv_l1, state_ssm_l1,
             state_s5_re_l2, state_s5_im_l2, cache_k_l3, cache_v_l3,
             norm_l0, attn_w_in_l0, lam_q1_l0, lam_k1_l0, lam_q2_l0, lam_k2_l0, subln_l0, attn_w_out_l0,
             norm_l1, ssm_w_in_l1, conv_w_l1, conv_b_l1, dt_bias_l1, a_log_l1, d_l1, gnorm_l1, ssm_w_out_l1,
             norm_l2, s5_w_in_l2, lam_re_l2, lam_im_l2, log_step_l2, b_re_l2, b_im_l2, c_re_l2, c_im_l2,
             d_l2, glu_w_l2, glu_b_l2, s5_w_out_l2,
             norm_l3, attn_w_in_l3, lam_q1_l3, lam_k1_l3, lam_q2_l3, lam_k2_l3, subln_l3, attn_w_out_l3,
             final_norm):
    D = cfg.d_model
    L, B, Ls, P = cfg.seq, cfg.dec_batch, cfg.dec_seq, cfg.past_len
    H, dv = cfg.a_heads, cfg.a_vdim
    xp = x_prompt.reshape(L, D).astype(F32)
    xs = x_sample.reshape(B * Ls, D).astype(F32)
    ropes = (_rope_tables(cfg, L, 0, L, "rope_p"), _rope_tables(cfg, B * Ls, P, Ls, "rope_s"))

    (xp, kp0, vp0), (xs, ks0, vs0) = _attn_layer(
        cfg, 0, xp, xs, cache_k_l0, cache_v_l0, norm_l0, attn_w_in_l0, lam_q1_l0, lam_k1_l0, lam_q2_l0,
        lam_k2_l0, subln_l0, attn_w_out_l0, ropes)
    (xp, conv_p1, ssm_p1), (xs, conv_s1, ssm_s1) = _ssd_layer(
        cfg, 1, xp, xs, state_conv_l1, state_ssm_l1, norm_l1, ssm_w_in_l1, conv_w_l1, conv_b_l1, dt_bias_l1,
        a_log_l1, d_l1, gnorm_l1, ssm_w_out_l1)
    (xp, s5re_p2, s5im_p2), (xs, s5re_s2, s5im_s2) = _s5_layer(
        cfg, 2, xp, xs, state_s5_re_l2, state_s5_im_l2, norm_l2, s5_w_in_l2, lam_re_l2, lam_im_l2,
        log_step_l2, b_re_l2, b_im_l2, c_re_l2, c_im_l2, d_l2, glu_w_l2, glu_b_l2, s5_w_out_l2)
    (xp, kp3, vp3), (xs, ks3, vs3) = _attn_layer(
        cfg, 3, xp, xs, cache_k_l3, cache_v_l3, norm_l3, attn_w_in_l3, lam_q1_l3, lam_k1_l3, lam_q2_l3,
        lam_k2_l3, subln_l3, attn_w_out_l3, ropes)

    y_prompt = _rmsnorm(xp, final_norm, F32, cfg.rms_eps, "final_p").reshape(1, L, D)
    y_sample = _rmsnorm(xs, final_norm, F32, cfg.rms_eps, "final_s").reshape(B, Ls, D)

    def kv_p(a):
        return a.reshape(1, L, H, dv)

    def kv_s(a):
        return a.reshape(B, Ls, H, dv)

    return (y_prompt, y_sample, kv_p(kp0), kv_p(vp0), kv_s(ks0), kv_s(vs0),
            conv_p1, ssm_p1, conv_s1, ssm_s1, s5re_p2, s5im_p2, s5re_s2, s5im_s2,
            kv_p(kp3), kv_p(vp3), kv_s(ks3), kv_s(vs3))


_CFG = Cfg()


def kernel(x_prompt, x_sample, cache_k_l0, cache_v_l0, state_conv_l1, state_ssm_l1, state_s5_re_l2, state_s5_im_l2, cache_k_l3, cache_v_l3, norm_l0, attn_w_in_l0, lam_q1_l0, lam_k1_l0, lam_q2_l0, lam_k2_l0, subln_l0, attn_w_out_l0, norm_l1, ssm_w_in_l1, conv_w_l1, conv_b_l1, dt_bias_l1, a_log_l1, d_l1, gnorm_l1, ssm_w_out_l1, norm_l2, s5_w_in_l2, lam_re_l2, lam_im_l2, log_step_l2, b_re_l2, b_im_l2, c_re_l2, c_im_l2, d_l2, glu_w_l2, glu_b_l2, s5_w_out_l2, norm_l3, attn_w_in_l3, lam_q1_l3, lam_k1_l3, lam_q2_l3, lam_k2_l3, subln_l3, attn_w_out_l3, final_norm):
    return _forward(_CFG, x_prompt, x_sample, cache_k_l0, cache_v_l0, state_conv_l1, state_ssm_l1, state_s5_re_l2, state_s5_im_l2, cache_k_l3, cache_v_l3, norm_l0, attn_w_in_l0, lam_q1_l0, lam_k1_l0, lam_q2_l0, lam_k2_l0, subln_l0, attn_w_out_l0, norm_l1, ssm_w_in_l1, conv_w_l1, conv_b_l1, dt_bias_l1, a_log_l1, d_l1, gnorm_l1, ssm_w_out_l1, norm_l2, s5_w_in_l2, lam_re_l2, lam_im_l2, log_step_l2, b_re_l2, b_im_l2, c_re_l2, c_im_l2, d_l2, glu_w_l2, glu_b_l2, s5_w_out_l2, norm_l3, attn_w_in_l3, lam_q1_l3, lam_k1_l3, lam_q2_l3, lam_k2_l3, subln_l3, attn_w_out_l3, final_norm)
```

```python
import dataclasses
import functools
import math

import jax
import jax.numpy as jnp
from jax import lax
from jax.experimental import pallas as pl
from jax.experimental.pallas import tpu as pltpu

F32 = jnp.float32
BF16 = jnp.bfloat16

LANES = 128
SUBLANES = 8
VMEM_LIMIT_BYTES = 48 * 1024 * 1024


@dataclasses.dataclass(frozen=True)
class Cfg:
    d_model: int = 4096
    seq: int = 16384
    dec_batch: int = 16
    dec_seq: int = 16
    past_len: int = 1024
    chunk: int = 64
    rms_eps: float = 1e-6
    rope_theta: float = 10000.0
    a_heads: int = 16
    a_head_dim: int = 128
    b_head_dim: int = 64
    b_groups: int = 8
    b_state: int = 128
    b_conv: int = 4
    c_group: int = 16
    c_state: int = 64
    attn_tk: int = 512
    ssd_chunk: int = 128
    s5_seg: int = 256
    s5_tt: int = 128
    mm_tm: int = 1024
    mm_tn: int = 512

    @property
    def a_vdim(self):
        return 2 * self.a_head_dim

    @property
    def a_width(self):
        return self.a_heads * self.a_vdim

    @property
    def b_width(self):
        return 2 * self.d_model

    @property
    def b_heads(self):
        return self.b_width // self.b_head_dim

    @property
    def b_conv_dim(self):
        return self.b_width + 2 * self.b_groups * self.b_state

    @property
    def c_width(self):
        return self.d_model

    @property
    def c_groups(self):
        return self.c_width // self.c_group


def _cp(sem):
    return pltpu.CompilerParams(dimension_semantics=sem, vmem_limit_bytes=VMEM_LIMIT_BYTES)


def _tile(n, pref, mult):
    if n <= pref:
        return n
    t = (pref // mult) * mult
    while t > mult and n % t:
        t -= mult
    assert n % t == 0, (n, pref, mult)
    return t


def _silu(x):
    return x / (1.0 + jnp.exp(-x))


def _rmsnorm_kernel(x_ref, w_ref, o_ref, *, eps):
    x = x_ref[...].astype(F32)
    ms = jnp.mean(x * x, axis=-1, keepdims=True)
    o_ref[...] = (x * lax.rsqrt(ms + eps) * w_ref[...]).astype(o_ref.dtype)


def _rmsnorm(x, w, out_dtype, eps, name):
    R, D = x.shape
    tr = _tile(R, 256, 8)
    return pl.pallas_call(
        functools.partial(_rmsnorm_kernel, eps=eps),
        grid=(R // tr,),
        in_specs=[pl.BlockSpec((tr, D), lambda i: (i, 0)), pl.BlockSpec((1, D), lambda i: (0, 0))],
        out_specs=pl.BlockSpec((tr, D), lambda i: (i, 0)),
        out_shape=jax.ShapeDtypeStruct((R, D), out_dtype),
        compiler_params=_cp(("parallel",)),
        name=name,
    )(x, w.reshape(1, D).astype(F32))


def _gated_rmsnorm_kernel(y_ref, z_ref, w_ref, o_ref, *, eps):
    y = y_ref[...].astype(F32) * _silu(z_ref[...].astype(F32))
    ms = jnp.mean(y * y, axis=-1, keepdims=True)
    o_ref[...] = (y * lax.rsqrt(ms + eps) * w_ref[...]).astype(o_ref.dtype)


def _gated_rmsnorm(y, z, w, eps, name):
    R, D = y.shape
    tr = _tile(R, 128, 8)
    return pl.pallas_call(
        functools.partial(_gated_rmsnorm_kernel, eps=eps),
        grid=(R // tr,),
        in_specs=[pl.BlockSpec((tr, D), lambda i: (i, 0)), pl.BlockSpec((tr, D), lambda i: (i, 0)),
                  pl.BlockSpec((1, D), lambda i: (0, 0))],
        out_specs=pl.BlockSpec((tr, D), lambda i: (i, 0)),
        out_shape=jax.ShapeDtypeStruct((R, D), BF16),
        compiler_params=_cp(("parallel",)),
        name=name,
    )(y, z, w.reshape(1, D).astype(F32))


def _mm_kernel(*refs, n_extra, epi):
    a_ref, w_ref = refs[0], refs[1]
    extras = refs[2:2 + n_extra]
    outs = refs[2 + n_extra:]
    acc = jnp.dot(a_ref[...], w_ref[...], preferred_element_type=F32)
    res = epi(acc, *[e[...] for e in extras])
    for o, r in zip(outs, res):
        if isinstance(r, (list, tuple)):
            for idx, piece in enumerate(r):
                o[idx] = piece.astype(o.dtype)
        else:
            o[...] = r.astype(o.dtype)


def _matmul(a, w, col0, ncols, out_dtypes, epi, extras, tm, tn, name):
    R, K = a.shape
    tm = _tile(R, tm, 8)
    tn = _tile(ncols, tn, LANES)
    assert col0 % tn == 0
    cb0 = col0 // tn
    in_specs = [pl.BlockSpec((tm, K), lambda i, j: (i, 0)),
                pl.BlockSpec((K, tn), lambda i, j: (0, cb0 + j))]
    args = [a, w]
    for arr, kind in extras:
        if kind == "row":
            in_specs.append(pl.BlockSpec((tm, arr.shape[1]), lambda i, j: (i, 0)))
        elif kind == "tile":
            in_specs.append(pl.BlockSpec((tm, tn), lambda i, j: (i, j)))
        elif kind == "col":
            in_specs.append(pl.BlockSpec((1, tn), lambda i, j: (0, j)))
        else:
            raise ValueError(kind)
        args.append(arr)
    out_specs, out_shape = [], []
    for dt in out_dtypes:
        if isinstance(dt, tuple):
            out_shape.append(dt[0])
            out_specs.append(dt[1](tm, tn))
        else:
            out_shape.append(jax.ShapeDtypeStruct((R, ncols), dt))
            out_specs.append(pl.BlockSpec((tm, tn), lambda i, j: (i, j)))
    return pl.pallas_call(
        functools.partial(_mm_kernel, n_extra=len(extras), epi=epi),
        grid=(R // tm, ncols // tn),
        in_specs=in_specs, out_specs=out_specs, out_shape=out_shape,
        compiler_params=_cp(("parallel", "arbitrary")),
        name=name,
    )(*args)


def _epi_plain(acc):
    return (acc,)


def _epi_two(acc):
    return (acc, acc)


def _rope_tile(acc, cos, sin):
    parts = []
    for c in range(acc.shape[1] // LANES):
        x = acc[:, c * LANES:(c + 1) * LANES]
        parts.append(x * cos + pltpu.roll(x, LANES // 2, axis=1) * sin)
    return parts[0] if len(parts) == 1 else jnp.concatenate(parts, axis=1)


def _epi_v_transposed(acc, *, tk, dv, pad):
    ones = jnp.ones((pad, tk), F32)
    pieces = []
    for r in range(acc.shape[0] // tk):
        blk = acc[r * tk:(r + 1) * tk, :]
        rows = []
        for hh in range(acc.shape[1] // dv):
            rows += [blk[:, hh * dv:(hh + 1) * dv].T, ones]
        pieces.append(jnp.concatenate(rows, axis=0))
    return (acc, pieces)


def _epi_rope_q(acc, cos, sin, *, scale):
    return (_rope_tile(acc, cos, sin) * scale,)


def _epi_rope_k(acc, cos, sin):
    r = _rope_tile(acc, cos, sin)
    return (r, r)


def _epi_residual(acc, x):
    return (x + acc,)


def _epi_softplus_bias(acc, b):
    v = acc + b
    return (jnp.maximum(v, 0.0) + jnp.log(1.0 + jnp.exp(-jnp.abs(v))),)


def _epi_glu(acc, b, yg, g):
    ygf = yg.astype(F32)
    sig = 1.0 / (1.0 + jnp.exp(-(acc + b)))
    return (ygf * sig * _silu(g.astype(F32)),)


def _rope_table_kernel(inv_ref, cos_ref, sin_ref, *, tr, pos0, period):
    i = pl.program_id(0)
    row = lax.broadcasted_iota(jnp.int32, (tr, LANES), 0) + i * tr
    pos = (pos0 + row % period).astype(F32)
    ang = pos * inv_ref[...]
    lane = lax.broadcasted_iota(jnp.int32, (tr, LANES), 1)
    cos_ref[...] = jnp.cos(ang)
    sin_ref[...] = jnp.where(lane < LANES // 2, -jnp.sin(ang), jnp.sin(ang))


def _rope_tables(cfg, rows, pos0, period, name):
    dh = cfg.a_head_dim
    inv = 1.0 / (cfg.rope_theta ** (jnp.arange(0, dh, 2, dtype=F32) / dh))
    inv2 = jnp.concatenate([inv, inv]).reshape(1, dh)
    tr = _tile(rows, 512, 8)
    return pl.pallas_call(
        functools.partial(_rope_table_kernel, tr=tr, pos0=pos0, period=period),
        grid=(rows // tr,),
        in_specs=[pl.BlockSpec((1, dh), lambda i: (0, 0))],
        out_specs=[pl.BlockSpec((tr, dh), lambda i: (i, 0))] * 2,
        out_shape=[jax.ShapeDtypeStruct((rows, dh), F32)] * 2,
        compiler_params=_cp(("parallel",)),
        name=name,
    )(inv2)


def _lambda(lamv, lam_init):
    s1 = jnp.sum(lamv[0:1, :] * lamv[1:2, :], axis=-1, keepdims=True)
    s2 = jnp.sum(lamv[2:3, :] * lamv[3:4, :], axis=-1, keepdims=True)
    return jnp.exp(s1) - jnp.exp(s2) + lam_init


def _attn_finish(o, g, sw, lam_init, eps):
    ms = jnp.mean(o * o, axis=-1, keepdims=True)
    on = (o * lax.rsqrt(ms + eps) * sw) * (1.0 - lam_init)
    return (on * _silu(g.astype(F32))).astype(BF16)


def _attn_prompt_kernel(lamv_ref, q_ref, k_ref, vt_ref, g_ref, sw_ref, o_ref, m_sc, acc_sc, sa_sc, sb_sc,
                        *, t, dh, dv, chunk, lam_init, eps):
    qi = pl.program_id(1)
    q = q_ref[...]
    m_sc[...] = jnp.full(m_sc.shape, -jnp.inf, F32)
    acc_sc[...] = jnp.zeros(acc_sc.shape, F32)

    def scores(kb, s_sc):
        kblk = k_ref[pl.ds(pl.multiple_of(kb * t, t), t), :]
        for mp in range(2):
            s_sc[mp] = lax.dot_general(kblk[:, mp * dh:(mp + 1) * dh], q[:, mp * dh:(mp + 1) * dh],
                                       (((1,), (1,)), ((), ())), preferred_element_type=F32)

    def consume(kb, s_sc, masked):
        vt = vt_ref[kb]
        if masked:
            keys = lax.broadcasted_iota(jnp.int32, (t, t), 0)
            qs = lax.broadcasted_iota(jnp.int32, (t, t), 1)
            visible = (keys // chunk) <= (qs // chunk)
        for mp in range(2):
            st = s_sc[mp]
            if masked:
                st = jnp.where(visible, st, -jnp.inf)
            m_prev = m_sc[mp]
            m_new = jnp.maximum(m_prev, jnp.max(st, axis=0, keepdims=True))
            alpha = jnp.exp2(m_prev - m_new)
            p = jnp.exp2(st - m_new)
            acc_sc[mp] = alpha * acc_sc[mp] + jnp.dot(vt, p.astype(BF16), preferred_element_type=F32)
            m_sc[mp] = m_new

    nfull = qi
    scores(0, sa_sc)

    def pair(kb):
        consume(kb, sa_sc, False)
        scores(kb + 1, sb_sc)
        consume(kb + 1, sb_sc, False)
        scores(kb + 2, sa_sc)

    def quad_body(i, c):
        pair(4 * i)
        pair(4 * i + 2)
        return c

    def pair_body(i, c):
        pair(4 * (nfull // 4) + 2 * i)
        return c

    lax.fori_loop(0, nfull // 4, quad_body, 0)
    lax.fori_loop(0, (nfull % 4) // 2, pair_body, 0)

    @pl.when(nfull % 2 == 0)
    def _():
        consume(nfull, sa_sc, True)

    @pl.when(nfull % 2 == 1)
    def _():
        consume(nfull - 1, sa_sc, False)
        scores(nfull, sb_sc)
        consume(nfull, sb_sc, True)

    lam = _lambda(lamv_ref[...], lam_init)
    ot = (acc_sc[0, :dv, :] / acc_sc[0, dv:dv + 1, :]
          - lam * (acc_sc[1, :dv, :] / acc_sc[1, dv:dv + 1, :]))
    o_ref[...] = _attn_finish(ot.T, g_ref[...], sw_ref[...], lam_init, eps)


def _attn_prompt(cfg, q, k, vt3, g, lamv, sw, lam_init, name):
    L = q.shape[0]
    H, dv, dh = cfg.a_heads, cfg.a_vdim, cfg.a_head_dim
    tq = tk = vt3.shape[2]
    dva = vt3.shape[1] // H
    assert vt3.shape[0] * tk == L and tk % cfg.chunk == 0
    kern = functools.partial(_attn_prompt_kernel, t=tk, dh=dh, dv=dv, chunk=cfg.chunk,
                             lam_init=lam_init, eps=cfg.rms_eps)
    return pl.pallas_call(
        kern,
        grid=(H, L // tq),
        in_specs=[pl.BlockSpec((4, dh), lambda h, i: (0, 0)),
                  pl.BlockSpec((tq, dv), lambda h, i: (i, h)),
                  pl.BlockSpec((L, dv), lambda h, i: (0, h)),
                  pl.BlockSpec((L // tk, dva, tk), lambda h, i: (0, h, 0)),
                  pl.BlockSpec((tq, dv), lambda h, i: (i, h)),
                  pl.BlockSpec((1, dv), lambda h, i: (0, 0))],
        out_specs=pl.BlockSpec((tq, dv), lambda h, i: (i, h)),
        out_shape=jax.ShapeDtypeStruct((L, H * dv), BF16),
        scratch_shapes=[pltpu.VMEM((2, 1, tq), F32), pltpu.VMEM((2, dva, tq), F32),
                        pltpu.VMEM((2, tk, tq), F32), pltpu.VMEM((2, tk, tq), F32)],
        compiler_params=_cp(("parallel", "arbitrary")),
        name=name,
    )(lamv, q, k, vt3, g, sw)


def _attn_sample_kernel(lamv_ref, q_ref, kn_ref, vn_ref, kc_ref, vc_ref, g_ref, sw_ref, o_ref,
                        *, P, Ls, dh, hb, chunk, lam_init, eps):
    dv = 2 * dh
    qpos_p = P + lax.broadcasted_iota(jnp.int32, (Ls, P), 0)
    kpos_p = lax.broadcasted_iota(jnp.int32, (Ls, P), 1)
    vis_p = (kpos_p // chunk) <= (qpos_p // chunk)
    qpos_n = P + lax.broadcasted_iota(jnp.int32, (Ls, Ls), 0)
    kpos_n = P + lax.broadcasted_iota(jnp.int32, (Ls, Ls), 1)
    vis_n = (kpos_n // chunk) <= (qpos_n // chunk)
    lam = _lambda(lamv_ref[...], lam_init)
    dn = (((1,), (1,)), ((), ()))
    for hh in range(hb):
        c0 = hh * dv
        q = q_ref[:, c0:c0 + dv]
        kn = kn_ref[:, c0:c0 + dv]
        vn = vn_ref[:, c0:c0 + dv]
        kc = kc_ref[0, :, c0:c0 + dv].astype(BF16)
        vc = vc_ref[0, :, c0:c0 + dv].astype(BF16)
        outs = []
        for mp in range(2):
            qm = q[:, mp * dh:(mp + 1) * dh]
            sp = lax.dot_general(qm, kc[:, mp * dh:(mp + 1) * dh], dn, preferred_element_type=F32)
            sn = lax.dot_general(qm, kn[:, mp * dh:(mp + 1) * dh], dn, preferred_element_type=F32)
            sp = jnp.where(vis_p, sp, -jnp.inf)
            sn = jnp.where(vis_n, sn, -jnp.inf)
            m = jnp.maximum(jnp.max(sp, axis=-1, keepdims=True), jnp.max(sn, axis=-1, keepdims=True))
            pp = jnp.exp2(sp - m)
            pn = jnp.exp2(sn - m)
            l = jnp.sum(pp, axis=-1, keepdims=True) + jnp.sum(pn, axis=-1, keepdims=True)
            acc = (jnp.dot(pp.astype(BF16), vc, preferred_element_type=F32)
                   + jnp.dot(pn.astype(BF16), vn, preferred_element_type=F32))
            outs.append(acc / l)
        o_ref[:, c0:c0 + dv] = _attn_finish(outs[0] - lam * outs[1], g_ref[:, c0:c0 + dv], sw_ref[...],
                                            lam_init, eps)


def _attn_sample(cfg, q, k, v, g, kc, vc, lamv, sw, lam_init, name):
    B, Ls, P = cfg.dec_batch, cfg.dec_seq, cfg.past_len
    H, dv, dh = cfg.a_heads, cfg.a_vdim, cfg.a_head_dim
    hb = _tile(H, 4, 1)
    kern = functools.partial(_attn_sample_kernel, P=P, Ls=Ls, dh=dh, hb=hb, chunk=cfg.chunk,
                             lam_init=lam_init, eps=cfg.rms_eps)
    row = pl.BlockSpec((Ls, hb * dv), lambda b, h: (b, h))
    cache = pl.BlockSpec((1, P, hb * dv), lambda b, h: (b, 0, h))
    return pl.pallas_call(
        kern,
        grid=(B, H // hb),
        in_specs=[pl.BlockSpec((4, dh), lambda b, h: (0, 0)), row, row, row, cache, cache, row,
                  pl.BlockSpec((1, dv), lambda b, h: (0, 0))],
        out_specs=row,
        out_shape=jax.ShapeDtypeStruct((B * Ls, H * dv), BF16),
        compiler_params=_cp(("parallel", "parallel")),
        name=name,
    )(lamv, q, k, v, kc.reshape(B, P, H * dv), vc.reshape(B, P, H * dv), g, sw)


def _attn_layer(cfg, li, xp, xs, cache_k, cache_v, nw, w_in, lq1, lk1, lq2, lk2, subln, w_out, ropes):
    lam_init = 0.8 - 0.6 * math.exp(-0.3 * li)
    W = cfg.a_width
    w_in_b = w_in.astype(BF16)
    w_out_b = w_out.astype(BF16)
    lamv = jnp.stack([lq1, lk1, lq2, lk2]).astype(F32)
    sw = subln.reshape(1, cfg.a_vdim).astype(F32)
    scale = cfg.a_head_dim ** -0.5 * math.log2(math.e)
    res = []
    for tag, x, (cos, sin) in (("p", xp, ropes[0]), ("s", xs, ropes[1])):
        nm = f"l{li}{tag}"
        h = _rmsnorm(x, nw, BF16, cfg.rms_eps, nm + "_norm")
        tm, tn = cfg.mm_tm, cfg.mm_tn
        rope_ex = [(cos, "row"), (sin, "row")]
        (q,) = _matmul(h, w_in_b, 0, W, [BF16], functools.partial(_epi_rope_q, scale=scale), rope_ex,
                       tm, tn, nm + "_q")
        k32, k16 = _matmul(h, w_in_b, W, W, [F32, BF16], _epi_rope_k, rope_ex, tm, tn, nm + "_k")
        (g,) = _matmul(h, w_in_b, 3 * W, W, [BF16], _epi_plain, [], tm, tn, nm + "_g")
        if tag == "p":
            L = x.shape[0]
            tk = _tile(L, cfg.attn_tk, LANES)
            dv, pad = cfg.a_vdim, 16
            dva = dv + pad
            vt_out = (jax.ShapeDtypeStruct((L // tk, cfg.a_heads * dva, tk), BF16),
                      lambda bm, bn: pl.BlockSpec((bm // tk, bn // dv * dva, tk), lambda i, j: (i, j, 0)))
            v32, vt3 = _matmul(h, w_in_b, 2 * W, W, [F32, vt_out],
                               functools.partial(_epi_v_transposed, tk=tk, dv=dv, pad=pad), [], tm, tn, nm + "_v")
            y = _attn_prompt(cfg, q, k16, vt3, g, lamv, sw, lam_init, nm + "_attn")
        else:
            v32, v16 = _matmul(h, w_in_b, 2 * W, W, [F32, BF16], _epi_two, [], tm, tn, nm + "_v")
            y = _attn_sample(cfg, q, k16, v16, g, cache_k, cache_v, lamv, sw, lam_init, nm + "_attn")
        (xn,) = _matmul(y, w_out_b, 0, cfg.d_model, [F32], _epi_residual, [(x, "tile")], tm, tn, nm + "_out")
        res.append((xn, k32, v32))
    return res


def _conv_kernel(x_ref, prev_ref, w_ref, b_ref, o_ref, carry_sc, *, tr, taps):
    r = pl.program_id(2)

    @pl.when(r == 0)
    def _():
        carry_sc[...] = prev_ref[0]

    x = x_ref[...].astype(F32)
    w = w_ref[...]
    b = b_ref[...]
    acc = b + x * w[taps - 1:taps, :]
    for j in range(1, taps):
        acc = acc + pltpu.roll(x, j, axis=0) * w[taps - 1 - j:taps - j, :]
    o_ref[...] = _silu(acc).astype(o_ref.dtype)
    x8 = x[0:SUBLANES, :]
    pr = carry_sc[...]
    row8 = lax.broadcasted_iota(jnp.int32, x8.shape, 0)
    acc8 = b + x8 * w[taps - 1:taps, :]
    for j in range(1, taps):
        sh = jnp.where(row8 < j, pltpu.roll(pr, j, axis=0), pltpu.roll(x8, j, axis=0))
        acc8 = acc8 + sh * w[taps - 1 - j:taps - j, :]
    o_ref[0:SUBLANES, :] = _silu(acc8).astype(o_ref.dtype)
    carry_sc[...] = x[tr - SUBLANES:tr, :]


def _conv_silu(xbc, prev8, w_t, b, nseq, name):
    R, C = xbc.shape
    taps = w_t.shape[0]
    Lseq = R // nseq
    tr = _tile(Lseq, 512, 16)
    tc = _tile(C, 512, LANES)
    nr = Lseq // tr
    return pl.pallas_call(
        functools.partial(_conv_kernel, tr=tr, taps=taps),
        grid=(C // tc, nseq, nr),
        in_specs=[pl.BlockSpec((tr, tc), lambda c, s, r: (s * nr + r, c)),
                  pl.BlockSpec((1, SUBLANES, tc), lambda c, s, r: (s, 0, c)),
                  pl.BlockSpec((taps, tc), lambda c, s, r: (0, c)),
                  pl.BlockSpec((1, tc), lambda c, s, r: (0, c))],
        out_specs=pl.BlockSpec((tr, tc), lambda c, s, r: (s * nr + r, c)),
        out_shape=jax.ShapeDtypeStruct((R, C), BF16),
        scratch_shapes=[pltpu.VMEM((SUBLANES, tc), F32)],
        compiler_params=_cp(("parallel", "arbitrary", "arbitrary")),
        name=name,
    )(xbc, prev8, w_t, b)


def _split_bf16(v):
    hi = v.astype(BF16)
    lo = (v - hi.astype(F32)).astype(BF16)
    return hi, lo


def _expand(v, e):
    hi, lo = _split_bf16(v)
    return (jnp.dot(hi, e, preferred_element_type=F32) + jnp.dot(lo, e, preferred_element_type=F32))


def _ssd_kernel(*refs, T, gw, hg, hd, N, has_init):
    if has_init:
        x_ref, b_ref, c_ref, dt_ref, alog_ref, d_ref, e_ref, init_ref, y_ref, st_ref, s_sc = refs
    else:
        x_ref, b_ref, c_ref, dt_ref, alog_ref, d_ref, e_ref, y_ref, st_ref, s_sc = refs
    ci = pl.program_id(2)

    @pl.when(ci == 0)
    def _():
        if has_init:
            s_sc[...] = init_ref[0].reshape(gw, N).T
        else:
            s_sc[...] = jnp.zeros(s_sc.shape, F32)

    dt = dt_ref[...]
    a_neg = -jnp.exp(alog_ref[...])
    a = dt * a_neg
    row = lax.broadcasted_iota(jnp.int32, a.shape, 0)
    d = 1
    while d < T:
        a = a + jnp.where(row >= d, pltpu.roll(a, d, axis=0), 0.0)
        d *= 2
    a_last = a[T - 1:T, :]
    e = e_ref[...]
    expa = jnp.exp(a)
    expa_x = jnp.dot(expa.astype(BF16), e, preferred_element_type=F32)
    dtdecay_x = jnp.dot((dt * jnp.exp(a_last - a)).astype(BF16), e, preferred_element_type=F32)
    exp_last_x = _expand(expa[T - SUBLANES:T, :], e)[SUBLANES - 1:SUBLANES, :]
    xb = x_ref[...]
    x = xb.astype(F32)
    bm = b_ref[...]
    cm = c_ref[...]
    cb = lax.dot_general(cm, bm, (((1,), (1,)), ((), ())), preferred_element_type=F32)
    s0 = s_sc[...]
    y_off = jnp.dot(cm, s0.astype(BF16), preferred_element_type=F32) * expa_x
    a2_t = (a - jnp.log(dt)).T
    tri = (lax.broadcasted_iota(jnp.int32, (T, T), 0) >= lax.broadcasted_iota(jnp.int32, (T, T), 1))
    lane = lax.broadcasted_iota(jnp.int32, (T, LANES), 1)
    per_pair = LANES // hd
    pieces = []
    for pi in range(gw // LANES):
        xp = xb[:, pi * LANES:(pi + 1) * LANES]
        piece = None
        for e_i in range(per_pair):
            hl = pi * per_pair + e_i
            seg = a[:, hl:hl + 1] - a2_t[hl:hl + 1, :]
            m = (cb * jnp.where(tri, jnp.exp(seg), 0.0)).astype(BF16)
            r = jnp.dot(m, xp, preferred_element_type=F32)
            if piece is None:
                piece = r
            else:
                piece = jnp.where(lane < e_i * hd, piece, r)
        pieces.append(piece)
    y_diag = pieces[0] if len(pieces) == 1 else jnp.concatenate(pieces, axis=1)
    y_ref[...] = (y_diag + y_off + d_ref[...] * x).astype(y_ref.dtype)
    xw = (x * dtdecay_x).astype(BF16)
    upd = lax.dot_general(bm, xw, (((0,), (0,)), ((), ())), preferred_element_type=F32)
    s_new = s0 * exp_last_x + upd
    s_sc[...] = s_new

    @pl.when(ci == pl.num_programs(2) - 1)
    def _():
        st_ref[0] = s_new.T.reshape(hg, hd, N)


def _ssd(cfg, xbc_act, dtg, alog_g, d_exp, e_mat, init, nseq, T, name):
    R = xbc_act.shape[0]
    G, N, hd = cfg.b_groups, cfg.b_state, cfg.b_head_dim
    hg = cfg.b_heads // G
    gw = hg * hd
    Lseq = R // nseq
    nc = Lseq // T
    bw = cfg.b_width
    boff = bw // N
    has_init = init is not None
    kern = functools.partial(_ssd_kernel, T=T, gw=gw, hg=hg, hd=hd, N=N, has_init=has_init)
    in_specs = [pl.BlockSpec((T, gw), lambda s, g, c: (s * nc + c, g)),
                pl.BlockSpec((T, N), lambda s, g, c: (s * nc + c, boff + g)),
                pl.BlockSpec((T, N), lambda s, g, c: (s * nc + c, boff + G + g)),
                pl.BlockSpec((T, LANES), lambda s, g, c: (s * nc + c, g)),
                pl.BlockSpec((1, LANES), lambda s, g, c: (0, g)),
                pl.BlockSpec((1, gw), lambda s, g, c: (0, g)),
                pl.BlockSpec((LANES, gw), lambda s, g, c: (0, 0))]
    args = [xbc_act, xbc_act, xbc_act, dtg, alog_g, d_exp, e_mat]
    if has_init:
        in_specs.append(pl.BlockSpec((1, hg, hd, N), lambda s, g, c: (s, g, 0, 0)))
        args.append(init)
    return pl.pallas_call(
        kern,
        grid=(nseq, G, nc),
        in_specs=in_specs,
        out_specs=[pl.BlockSpec((T, gw), lambda s, g, c: (s * nc + c, g)),
                   pl.BlockSpec((1, hg, hd, N), lambda s, g, c: (s, g, 0, 0))],
        out_shape=[jax.ShapeDtypeStruct((R, bw), BF16),
                   jax.ShapeDtypeStruct((nseq, cfg.b_heads, hd, N), F32)],
        scratch_shapes=[pltpu.VMEM((N, gw), F32)],
        compiler_params=_cp(("parallel", "parallel", "arbitrary")),
        name=name,
    )(*args)


def _ssd_layer(cfg, li, xp, xs, conv_prev, ssm_prev, nw, w_in, conv_w, conv_b, dt_bias, a_log, d_skip,
               gnorm, w_out):
    bw, cd, H, G = cfg.b_width, cfg.b_conv_dim, cfg.b_heads, cfg.b_groups
    hg = H // G
    hd = cfg.b_head_dim
    gw = hg * hd
    w_in_b = w_in.astype(BF16)
    w_out_b = w_out.astype(BF16)

    def per_group(v):
        lead = v.shape[:-1]
        v = v.reshape(lead + (G, hg))
        v = jnp.pad(v, [(0, 0)] * len(lead) + [(0, 0), (0, LANES - hg)])
        return v.reshape(lead + (G * LANES,))

    w_dt_g = per_group(w_in[:, bw + cd:]).astype(BF16)
    dtb_g = per_group(dt_bias.astype(F32)).reshape(1, G * LANES)
    alog_g = per_group(a_log.astype(F32)).reshape(1, G * LANES)
    d_exp = jnp.repeat(d_skip.astype(F32), hd).reshape(1, bw)
    e_mat = (jnp.arange(LANES)[:, None] == (jnp.arange(gw)[None, :] // hd)).astype(BF16)
    w_t = conv_w.astype(F32).T
    cb2 = conv_b.astype(F32).reshape(1, cd)
    T = cfg.ssd_chunk
    B, Ls = cfg.dec_batch, cfg.dec_seq
    res = []
    for tag, x in (("p", xp), ("s", xs)):
        nm = f"l{li}{tag}"
        tm, tn = cfg.mm_tm, cfg.mm_tn
        h = _rmsnorm(x, nw, BF16, cfg.rms_eps, nm + "_norm")
        (z,) = _matmul(h, w_in_b, 0, bw, [BF16], _epi_plain, [], tm, tn, nm + "_z")
        (xbc,) = _matmul(h, w_in_b, bw, cd, [BF16], _epi_plain, [], tm, tn, nm + "_xbc")
        (dtg,) = _matmul(h, w_dt_g, 0, G * LANES, [F32], _epi_softplus_bias, [(dtb_g, "col")], tm, tn,
                         nm + "_dt")
        if tag == "p":
            prev8 = jnp.zeros((1, SUBLANES, cd), F32)
            act = _conv_silu(xbc, prev8, w_t, cb2, 1, nm + "_conv")
            conv_new = xbc[x.shape[0] - (cfg.b_conv - 1):].astype(F32)[None]
            y, st = _ssd(cfg, act, dtg, alog_g, d_exp, e_mat, None, 1, T, nm + "_ssd")
        else:
            prev8 = jnp.pad(conv_prev.astype(F32), ((0, 0), (SUBLANES - (cfg.b_conv - 1), 0), (0, 0)))
            act = _conv_silu(xbc, prev8, w_t, cb2, B, nm + "_conv")
            conv_new = xbc.reshape(B, Ls, cd)[:, Ls - (cfg.b_conv - 1):].astype(F32)
            act_p = jnp.pad(act.reshape(B, Ls, cd), ((0, 0), (0, T - Ls), (0, 0))).reshape(B * T, cd)
            dt_p = jnp.pad(dtg.reshape(B, Ls, G * LANES), ((0, 0), (0, T - Ls), (0, 0))).reshape(B * T, G * LANES)
            y_p, st = _ssd(cfg, act_p, dt_p, alog_g, d_exp, e_mat, ssm_prev.astype(F32), B, T, nm + "_ssd")
            y = y_p.reshape(B, T, bw)[:, :Ls].reshape(B * Ls, bw)
        yn = _gated_rmsnorm(y, z, gnorm, cfg.rms_eps, nm + "_gnorm")
        (xn,) = _matmul(yn, w_out_b, 0, cfg.d_model, [F32], _epi_residual, [(x, "tile")], tm // 2, tn,
                        nm + "_out")
        res.append((xn, conv_new, st))
    return res


def _s5_disc_kernel(lr_ref, li_ref, ls_ref, br_ref, bi_ref, ar_ref, ai_ref, bbr_ref, bbi_ref):
    lr = lr_ref[...]
    li = li_ref[...]
    step = jnp.exp(ls_ref[...])
    mag = jnp.exp(lr * step)
    ar = mag * jnp.cos(li * step)
    ai = mag * jnp.sin(li * step)
    den = lr * lr + li * li
    nr = ar - 1.0
    ni = ai
    cr = ((nr * lr + ni * li) / den)[:, None, :]
    ci = ((ni * lr - nr * li) / den)[:, None, :]
    br = br_ref[...]
    bi = bi_ref[...]
    ar_ref[...] = ar
    ai_ref[...] = ai
    bbr_ref[...] = cr * br - ci * bi
    bbi_ref[...] = cr * bi + ci * br


def _s5_discretize(cfg, lam_re, lam_im, log_step, b_re, b_im, name):
    G, P, K = cfg.c_groups, cfg.c_state, cfg.c_group
    bt_r = jnp.transpose(b_re.astype(F32), (0, 2, 1))
    bt_i = jnp.transpose(b_im.astype(F32), (0, 2, 1))
    tg = _tile(G, 64, 8)
    s2 = pl.BlockSpec((tg, P), lambda i: (i, 0))
    s3 = pl.BlockSpec((tg, K, P), lambda i: (i, 0, 0))
    return pl.pallas_call(
        _s5_disc_kernel,
        grid=(G // tg,),
        in_specs=[s2, s2, pl.BlockSpec((tg, 1), lambda i: (i, 0)), s3, s3],
        out_specs=[s2, s2, s3, s3],
        out_shape=[jax.ShapeDtypeStruct((G, P), F32)] * 2 + [jax.ShapeDtypeStruct((G, K, P), F32)] * 2,
        compiler_params=_cp(("parallel",)),
        name=name,
    )(lam_re.astype(F32), lam_im.astype(F32), log_step.astype(F32).reshape(G, 1), bt_r, bt_i)


def _gelu_tanh(v):
    return 0.5 * v * (1.0 + jnp.tanh(math.sqrt(2.0 / math.pi) * (v + 0.044715 * (v * v * v))))


def _s5_scan_kernel(*refs, tT, pitch, nsub, sw, emit_y):
    if emit_y:
        (u_ref, bbd_ref, cbd_ref, ar_ref, ai_ref, d_ref, hr0_ref, hi0_ref,
         yg_ref, hrT_ref, hiT_ref, bu_a, bu_b, hs_a, hs_b, hr_sc, hi_sc) = refs
    else:
        (u_ref, bbd_ref, ar_ref, ai_ref, hr0_ref, hi0_ref,
         hrT_ref, hiT_ref, bu_a, bu_b, hr_sc, hi_sc) = refs
        hs_a = hs_b = None
    tb = pl.program_id(2)

    @pl.when(tb == 0)
    def _():
        hr_sc[...] = hr0_ref[...]
        hi_sc[...] = hi0_ref[...]

    nl = sw // LANES
    ar = jnp.broadcast_to(ar_ref[...], (SUBLANES, sw))
    ai = jnp.broadcast_to(ai_ref[...], (SUBLANES, sw))
    hr = hr_sc[...]
    hi = hi_sc[...]
    ts = tT // nsub
    bufs = ((bu_a, hs_a), (bu_b, hs_b))

    def in_map(sb):
        bu_sc = bufs[sb % 2][0]
        ub = u_ref[:, sb * ts:(sb + 1) * ts, :].reshape(SUBLANES * ts, LANES)
        bu = jnp.dot(ub, bbd_ref[0], preferred_element_type=F32)
        for l in range(2 * nl):
            for sq in range(SUBLANES):
                bu_sc[l, sq * pitch:sq * pitch + ts, :] = bu[sq * ts:(sq + 1) * ts, l * LANES:(l + 1) * LANES]

    def out_map(sb):
        hs_sc = bufs[sb % 2][1]
        ub = u_ref[:, sb * ts:(sb + 1) * ts, :].reshape(SUBLANES * ts, LANES)
        hs = jnp.concatenate(
            [jnp.concatenate([hs_sc[l, sq * pitch:sq * pitch + ts, :] for sq in range(SUBLANES)], axis=0)
             for l in range(2 * nl)], axis=1).astype(BF16)
        y = jnp.dot(hs, cbd_ref[0], preferred_element_type=F32)
        v = y + d_ref[...] * ub.astype(F32)
        yg_ref[:, sb * ts:(sb + 1) * ts, :] = _gelu_tanh(v).reshape(SUBLANES, ts, LANES).astype(yg_ref.dtype)

    in_map(0)
    for sb in range(nsub):
        bu_sc, hs_sc = bufs[sb % 2]
        if sb + 1 < nsub:
            in_map(sb + 1)
        if emit_y and sb >= 1:
            out_map(sb - 1)
        for t in range(ts):
            rows = pl.ds(t, SUBLANES, stride=pitch)
            bur = jnp.concatenate([bu_sc[l, rows, :] for l in range(nl)], axis=1)
            bui = jnp.concatenate([bu_sc[nl + l, rows, :] for l in range(nl)], axis=1)
            hr, hi = ar * hr - ai * hi + bur, ar * hi + ai * hr + bui
            if emit_y:
                for l in range(nl):
                    hs_sc[l, rows, :] = hr[:, l * LANES:(l + 1) * LANES]
                    hs_sc[nl + l, rows, :] = hi[:, l * LANES:(l + 1) * LANES]
    if emit_y:
        out_map(nsub - 1)
    hr_sc[...] = hr
    hi_sc[...] = hi

    @pl.when(tb == pl.num_programs(2) - 1)
    def _():
        hrT_ref[...] = hr
        hiT_ref[...] = hi


def _s5_scan(cfg, u3, bbd, cbd, ar, ai, d2, hr0, hi0, emit_y, name):
    NS, T, C = u3.shape
    nj = C // LANES
    sw = bbd.shape[2] // 2
    tT = _tile(T, cfg.s5_tt, 16)
    nsg = NS // SUBLANES
    nsub = max(1, tT // 16)
    ts = tT // nsub
    pitch = ts + SUBLANES if (ts // SUBLANES) % 2 == 0 else ts
    kern = functools.partial(_s5_scan_kernel, tT=tT, pitch=pitch, nsub=nsub, sw=sw, emit_y=emit_y)
    u_spec = pl.BlockSpec((SUBLANES, tT, LANES), lambda j, s, t: (s, t, j))
    bbd_spec = pl.BlockSpec((1, LANES, 2 * sw), lambda j, s, t: (j, 0, 0))
    a_spec = pl.BlockSpec((1, sw), lambda j, s, t: (0, j))
    st_spec = pl.BlockSpec((SUBLANES, sw), lambda j, s, t: (s, j))
    st_shape = jax.ShapeDtypeStruct((NS, nj * sw), F32)
    slab = pltpu.VMEM((2 * sw // LANES, SUBLANES * pitch, LANES), F32)
    scratch = [slab, slab]
    if emit_y:
        in_specs = [u_spec, bbd_spec, pl.BlockSpec((1, 2 * sw, LANES), lambda j, s, t: (j, 0, 0)),
                    a_spec, a_spec, pl.BlockSpec((1, LANES), lambda j, s, t: (0, j)), st_spec, st_spec]
        args = [u3, bbd, cbd, ar, ai, d2, hr0, hi0]
        out_specs = [u_spec, st_spec, st_spec]
        out_shape = [jax.ShapeDtypeStruct((NS, T, C), BF16), st_shape, st_shape]
        scratch += [slab, slab]
    else:
        in_specs = [u_spec, bbd_spec, a_spec, a_spec, st_spec, st_spec]
        args = [u3, bbd, ar, ai, hr0, hi0]
        out_specs = [st_spec, st_spec]
        out_shape = [st_shape, st_shape]
    scratch += [pltpu.VMEM((SUBLANES, sw), F32)] * 2
    return pl.pallas_call(
        kern,
        grid=(nj, nsg, T // tT),
        in_specs=in_specs, out_specs=out_specs, out_shape=out_shape,
        scratch_shapes=scratch,
        compiler_params=_cp(("parallel", "parallel", "arbitrary")),
        name=name,
    )(*args)


def _s5_combine_kernel(er_ref, ei_ref, ar_ref, ai_ref, h0r_ref, h0i_ref, hr_ref, hi_ref, *, nseg, nsq):
    pr = ar_ref[...]
    pi_ = ai_ref[...]
    for _ in range(nsq):
        pr, pi_ = pr * pr - pi_ * pi_, 2.0 * pr * pi_
    hr = h0r_ref[...]
    hi = h0i_ref[...]
    hr_ref[0:1, :] = hr
    hi_ref[0:1, :] = hi
    for i in range(1, nseg):
        er = er_ref[i - 1:i, :]
        ei = ei_ref[i - 1:i, :]
        hr, hi = pr * hr - pi_ * hi + er, pr * hi + pi_ * hr + ei
        hr_ref[i:i + 1, :] = hr
        hi_ref[i:i + 1, :] = hi


def _s5_combine(er, ei, ar, ai, h0r, h0i, seg_len, name):
    nseg, W = er.shape
    nsq = int(round(math.log2(seg_len)))
    assert 2 ** nsq == seg_len
    tw = _tile(W, 1024, LANES)
    full = pl.BlockSpec((nseg, tw), lambda i: (0, i))
    row = pl.BlockSpec((1, tw), lambda i: (0, i))
    return pl.pallas_call(
        functools.partial(_s5_combine_kernel, nseg=nseg, nsq=nsq),
        grid=(W // tw,),
        in_specs=[full, full, row, row, row, row],
        out_specs=[full, full],
        out_shape=[jax.ShapeDtypeStruct((nseg, W), F32)] * 2,
        compiler_params=_cp(("parallel",)),
        name=name,
    )(er, ei, ar, ai, h0r, h0i)


def _s5_layer(cfg, li, xp, xs, s_re, s_im, nw, w_in, lam_re, lam_im, log_step, b_re, b_im, c_re, c_im,
              d_skip, w_glu, b_glu, w_out):
    CW, G, P, K = cfg.c_width, cfg.c_groups, cfg.c_state, cfg.c_group
    gpb = LANES // K
    nj = CW // LANES
    sw = gpb * P
    w_in_b = w_in.astype(BF16)
    w_glu_b = w_glu.astype(BF16)
    w_out_b = w_out.astype(BF16)
    ar, ai, bbr, bbi = _s5_discretize(cfg, lam_re, lam_im, log_step, b_re, b_im, f"l{li}_disc")
    eye = jnp.eye(gpb, dtype=F32)
    bb = jnp.stack([bbr, bbi]).reshape(2, nj, gpb, K, P)
    bbd = jnp.einsum("rjgkp,gh->jgkrhp", bb, eye).reshape(nj, LANES, 2 * sw).astype(BF16)
    cc = jnp.stack([c_re.astype(F32), -c_im.astype(F32)]).reshape(2, nj, gpb, K, P)
    cbd = jnp.einsum("rjgkp,gh->jrgphk", cc, eye).reshape(nj, 2 * sw, LANES).astype(BF16)
    ar2 = ar.reshape(1, G * P)
    ai2 = ai.reshape(1, G * P)
    d2 = d_skip.astype(F32).reshape(1, CW)
    bg2 = b_glu.astype(F32).reshape(1, CW)
    B, Ls = cfg.dec_batch, cfg.dec_seq
    res = []
    for tag, x in (("p", xp), ("s", xs)):
        nm = f"l{li}{tag}"
        tm, tn = cfg.mm_tm, cfg.mm_tn
        h = _rmsnorm(x, nw, BF16, cfg.rms_eps, nm + "_norm")
        (u,) = _matmul(h, w_in_b, 0, CW, [BF16], _epi_plain, [], tm, tn, nm + "_u")
        (g,) = _matmul(h, w_in_b, CW, CW, [BF16], _epi_plain, [], tm, tn, nm + "_g")
        if tag == "p":
            L = x.shape[0]
            seg = _tile(L // SUBLANES, cfg.s5_seg, 16)
            nseg = L // seg
            u3 = u.reshape(nseg, seg, CW)
            zs = jnp.zeros((nseg, G * P), F32)
            er, ei = _s5_scan(cfg, u3, bbd, None, ar2, ai2, None, zs, zs, False, nm + "_scan_a")
            z1 = jnp.zeros((1, G * P), F32)
            h0r, h0i = _s5_combine(er, ei, ar2, ai2, z1, z1, seg, nm + "_comb")
            yg3, hrT, hiT = _s5_scan(cfg, u3, bbd, cbd, ar2, ai2, d2, h0r, h0i, True, nm + "_scan_b")
            yg = yg3.reshape(L, CW)
            st_r = hrT[nseg - 1:nseg].reshape(1, G, P)
            st_i = hiT[nseg - 1:nseg].reshape(1, G, P)
        else:
            u3 = u.reshape(B, Ls, CW)
            yg3, hrT, hiT = _s5_scan(cfg, u3, bbd, cbd, ar2, ai2, d2,
                                     s_re.astype(F32).reshape(B, G * P), s_im.astype(F32).reshape(B, G * P),
                                     True, nm + "_scan")
            yg = yg3.reshape(B * Ls, CW)
            st_r = hrT.reshape(B, G, P)
            st_i = hiT.reshape(B, G, P)
        (y2,) = _matmul(yg, w_glu_b, 0, CW, [BF16], _epi_glu, [(bg2, "col"), (yg, "tile"), (g, "tile")],
                        tm, tn, nm + "_glu")
        (xn,) = _matmul(y2, w_out_b, 0, cfg.d_model, [F32], _epi_residual, [(x, "tile")], tm, tn, nm + "_out")
        res.append((xn, st_r, st_i))
    return res


def _forward(cfg, x_prompt, x_sample,
             cache_k_l0, cache_v_l0, state_conv_l1, state_ssm_l1,
             state_s5_re_l2, state_s5_im_l2, cache_k_l3, cache_v_l3,
             norm_l0, attn_w_in_l0, lam_q1_l0, lam_k1_l0, lam_q2_l0, lam_k2_l0, subln_l0, attn_w_out_l0,
             norm_l1, ssm_w_in_l1, conv_w_l1, conv_b_l1, dt_bias_l1, a_log_l1, d_l1, gnorm_l1, ssm_w_out_l1,
             norm_l2, s5_w_in_l2, lam_re_l2, lam_im_l2, log_step_l2, b_re_l2, b_im_l2, c_re_l2, c_im_l2,
             d_l2, glu_w_l2, glu_b_l2, s5_w_out_l2,
             norm_l3, attn_w_in_l3, lam_q1_l3, lam_k1_l3, lam_q2_l3, lam_k2_l3, subln_l3, attn_w_out_l3,
             final_norm):
    D = cfg.d_model
    L, B, Ls, P = cfg.seq, cfg.dec_batch, cfg.dec_seq, cfg.past_len
    H, dv = cfg.a_heads, cfg.a_vdim
    xp = x_prompt.reshape(L, D).astype(F32)
    xs = x_sample.reshape(B * Ls, D).astype(F32)
    ropes = (_rope_tables(cfg, L, 0, L, "rope_p"), _rope_tables(cfg, B * Ls, P, Ls, "rope_s"))

    (xp, kp0, vp0), (xs, ks0, vs0) = _attn_layer(
        cfg, 0, xp, xs, cache_k_l0, cache_v_l0, norm_l0, attn_w_in_l0, lam_q1_l0, lam_k1_l0, lam_q2_l0,
        lam_k2_l0, subln_l0, attn_w_out_l0, ropes)
    (xp, conv_p1, ssm_p1), (xs, conv_s1, ssm_s1) = _ssd_layer(
        cfg, 1, xp, xs, state_conv_l1, state_ssm_l1, norm_l1, ssm_w_in_l1, conv_w_l1, conv_b_l1, dt_bias_l1,
        a_log_l1, d_l1, gnorm_l1, ssm_w_out_l1)
    (xp, s5re_p2, s5im_p2), (xs, s5re_s2, s5im_s2) = _s5_layer(
        cfg, 2, xp, xs, state_s5_re_l2, state_s5_im_l2, norm_l2, s5_w_in_l2, lam_re_l2, lam_im_l2,
        log_step_l2, b_re_l2, b_im_l2, c_re_l2, c_im_l2, d_l2, glu_w_l2, glu_b_l2, s5_w_out_l2)
    (xp, kp3, vp3), (xs, ks3, vs3) = _attn_layer(
        cfg, 3, xp, xs, cache_k_l3, cache_v_l3, norm_l3, attn_w_in_l3, lam_q1_l3, lam_k1_l3, lam_q2_l3,
        lam_k2_l3, subln_l3, attn_w_out_l3, ropes)

    y_prompt = _rmsnorm(xp, final_norm, F32, cfg.rms_eps, "final_p").reshape(1, L, D)
    y_sample = _rmsnorm(xs, final_norm, F32, cfg.rms_eps, "final_s").reshape(B, Ls, D)

    def kv_p(a):
        return a.reshape(1, L, H, dv)

    def kv_s(a):
        return a.reshape(B, Ls, H, dv)

    return (y_prompt, y_sample, kv_p(kp0), kv_p(vp0), kv_s(ks0), kv_s(vs0),
            conv_p1, ssm_p1, conv_s1, ssm_s1, s5re_p2, s5im_p2, s5re_s2, s5im_s2,
            kv_p(kp3), kv_p(vp3), kv_s(ks3), kv_s(vs3))


_CFG = Cfg()


def kernel(x_prompt, x_sample, cache_k_l0, cache_v_l0, state_conv_l1, state_ssm_l1, state_s5_re_l2, state_s5_im_l2, cache_k_l3, cache_v_l3, norm_l0, attn_w_in_l0, lam_q1_l0, lam_k1_l0, lam_q2_l0, lam_k2_l0, subln_l0, attn_w_out_l0, norm_l1, ssm_w_in_l1, conv_w_l1, conv_b_l1, dt_bias_l1, a_log_l1, d_l1, gnorm_l1, ssm_w_out_l1, norm_l2, s5_w_in_l2, lam_re_l2, lam_im_l2, log_step_l2, b_re_l2, b_im_l2, c_re_l2, c_im_l2, d_l2, glu_w_l2, glu_b_l2, s5_w_out_l2, norm_l3, attn_w_in_l3, lam_q1_l3, lam_k1_l3, lam_q2_l3, lam_k2_l3, subln_l3, attn_w_out_l3, final_norm):
    return _forward(_CFG, x_prompt, x_sample, cache_k_l0, cache_v_l0, state_conv_l1, state_ssm_l1, state_s5_re_l2, state_s5_im_l2, cache_k_l3, cache_v_l3, norm_l0, attn_w_in_l0, lam_q1_l0, lam_k1_l0, lam_q2_l0, lam_k2_l0, subln_l0, attn_w_out_l0, norm_l1, ssm_w_in_l1, conv_w_l1, conv_b_l1, dt_bias_l1, a_log_l1, d_l1, gnorm_l1, ssm_w_out_l1, norm_l2, s5_w_in_l2, lam_re_l2, lam_im_l2, log_step_l2, b_re_l2, b_im_l2, c_re_l2, c_im_l2, d_l2, glu_w_l2, glu_b_l2, s5_w_out_l2, norm_l3, attn_w_in_l3, lam_q1_l3, lam_k1_l3, lam_q2_l3, lam_k2_l3, subln_l3, attn_w_out_l3, final_norm)
```

```python
import dataclasses
import functools
import math

import jax
import jax.numpy as jnp
from jax import lax
from jax.experimental import pallas as pl
from jax.experimental.pallas import tpu as pltpu

F32 = jnp.float32
BF16 = jnp.bfloat16

LANES = 128
SUBLANES = 8
VMEM_LIMIT_BYTES = 48 * 1024 * 1024


@dataclasses.dataclass(frozen=True)
class Cfg:
    d_model: int = 4096
    seq: int = 16384
    dec_batch: int = 16
    dec_seq: int = 16
    past_len: int = 1024
    chunk: int = 64
    rms_eps: float = 1e-6
    rope_theta: float = 10000.0
    a_heads: int = 16
    a_head_dim: int = 128
    b_head_dim: int = 64
    b_groups: int = 8
    b_state: int = 128
    b_conv: int = 4
    c_group: int = 16
    c_state: int = 64
    attn_tk: int = 512
    ssd_chunk: int = 128
    s5_seg: int = 256
    s5_tt: int = 128
    mm_tm: int = 1024
    mm_tn: int = 512

    @property
    def a_vdim(self):
        return 2 * self.a_head_dim

    @property
    def a_width(self):
        return self.a_heads * self.a_vdim

    @property
    def b_width(self):
        return 2 * self.d_model

    @property
    def b_heads(self):
        return self.b_width // self.b_head_dim

    @property
    def b_conv_dim(self):
        return self.b_width + 2 * self.b_groups * self.b_state

    @property
    def c_width(self):
        return self.d_model

    @property
    def c_groups(self):
        return self.c_width // self.c_group


def _cp(sem):
    return pltpu.CompilerParams(dimension_semantics=sem, vmem_limit_bytes=VMEM_LIMIT_BYTES)


def _tile(n, pref, mult):
    if n <= pref:
        return n
    t = (pref // mult) * mult
    while t > mult and n % t:
        t -= mult
    assert n % t == 0, (n, pref, mult)
    return t


def _silu(x):
    return x / (1.0 + jnp.exp(-x))


def _rmsnorm_kernel(x_ref, w_ref, o_ref, *, eps):
    x = x_ref[...].astype(F32)
    ms = jnp.mean(x * x, axis=-1, keepdims=True)
    o_ref[...] = (x * lax.rsqrt(ms + eps) * w_ref[...]).astype(o_ref.dtype)


def _rmsnorm(x, w, out_dtype, eps, name):
    R, D = x.shape
    tr = _tile(R, 256, 8)
    return pl.pallas_call(
        functools.partial(_rmsnorm_kernel, eps=eps),
        grid=(R // tr,),
        in_specs=[pl.BlockSpec((tr, D), lambda i: (i, 0)), pl.BlockSpec((1, D), lambda i: (0, 0))],
        out_specs=pl.BlockSpec((tr, D), lambda i: (i, 0)),
        out_shape=jax.ShapeDtypeStruct((R, D), out_dtype),
        compiler_params=_cp(("parallel",)),
        name=name,
    )(x, w.reshape(1, D).astype(F32))


def _gated_rmsnorm_kernel(y_ref, z_ref, w_ref, o_ref, *, eps):
    y = y_ref[...].astype(F32) * _silu(z_ref[...].astype(F32))
    ms = jnp.mean(y * y, axis=-1, keepdims=True)
    o_ref[...] = (y * lax.rsqrt(ms + eps) * w_ref[...]).astype(o_ref.dtype)


def _gated_rmsnorm(y, z, w, eps, name):
    R, D = y.shape
    tr = _tile(R, 128, 8)
    return pl.pallas_call(
        functools.partial(_gated_rmsnorm_kernel, eps=eps),
        grid=(R // tr,),
        in_specs=[pl.BlockSpec((tr, D), lambda i: (i, 0)), pl.BlockSpec((tr, D), lambda i: (i, 0)),
                  pl.BlockSpec((1, D), lambda i: (0, 0))],
        out_specs=pl.BlockSpec((tr, D), lambda i: (i, 0)),
        out_shape=jax.ShapeDtypeStruct((R, D), BF16),
        compiler_params=_cp(("parallel",)),
        name=name,
    )(y, z, w.reshape(1, D).astype(F32))


@dataclasses.dataclass
class _HeadRows:
    tile: jax.Array
    dv: int


def _mm_kernel(*refs, n_extra, epi):
    a_ref, w_ref = refs[0], refs[1]
    extras = refs[2:2 + n_extra]
    outs = refs[2 + n_extra:]
    acc = jnp.dot(a_ref[...], w_ref[...], preferred_element_type=F32)
    res = epi(acc, *[e[...] for e in extras])
    for o, r in zip(outs, res):
        if isinstance(r, _HeadRows):
            for hh in range(r.tile.shape[1] // r.dv):
                o[:, hh, :] = r.tile[:, hh * r.dv:(hh + 1) * r.dv].astype(o.dtype)
        elif isinstance(r, (list, tuple)):
            for idx, piece in enumerate(r):
                o[idx] = piece.astype(o.dtype)
        else:
            o[...] = r.astype(o.dtype)


def _matmul(a, w, col0, ncols, out_dtypes, epi, extras, tm, tn, name, weights_outer=False):
    R, K = a.shape
    tm = _tile(R, tm, 8)
    tn = _tile(ncols, tn, LANES)
    assert col0 % tn == 0
    cb0 = col0 // tn

    def ix(f):
        return (lambda jj, ii: f(ii, jj)) if weights_outer else f

    w_kw = dict(pipeline_mode=pl.Buffered(1)) if weights_outer else {}
    in_specs = [pl.BlockSpec((tm, K), ix(lambda i, j: (i, 0))),
                pl.BlockSpec((K, tn), ix(lambda i, j: (0, cb0 + j)), **w_kw)]
    args = [a, w]
    for arr, kind in extras:
        if kind == "row":
            in_specs.append(pl.BlockSpec((tm, arr.shape[1]), ix(lambda i, j: (i, 0))))
        elif kind == "tile":
            in_specs.append(pl.BlockSpec((tm, tn), ix(lambda i, j: (i, j))))
        elif kind == "col":
            in_specs.append(pl.BlockSpec((1, tn), ix(lambda i, j: (0, j))))
        else:
            raise ValueError(kind)
        args.append(arr)
    out_specs, out_shape = [], []
    for dt in out_dtypes:
        if isinstance(dt, tuple):
            block, fn = dt[1](tm, tn)
            out_shape.append(dt[0])
            out_specs.append(pl.BlockSpec(block, ix(fn)))
        else:
            out_shape.append(jax.ShapeDtypeStruct((R, ncols), dt))
            out_specs.append(pl.BlockSpec((tm, tn), ix(lambda i, j: (i, j))))
    grid = (ncols // tn, R // tm) if weights_outer else (R // tm, ncols // tn)
    return pl.pallas_call(
        functools.partial(_mm_kernel, n_extra=len(extras), epi=epi),
        grid=grid,
        in_specs=in_specs, out_specs=out_specs, out_shape=out_shape,
        compiler_params=_cp(("arbitrary", "arbitrary") if weights_outer else ("parallel", "arbitrary")),
        name=name,
    )(*args)


def _epi_plain(acc):
    return (acc,)


def _epi_two(acc):
    return (acc, acc)


def _rope_tile(acc, cos, sin):
    parts = []
    for c in range(acc.shape[1] // LANES):
        x = acc[:, c * LANES:(c + 1) * LANES]
        parts.append(x * cos + pltpu.roll(x, LANES // 2, axis=1) * sin)
    return parts[0] if len(parts) == 1 else jnp.concatenate(parts, axis=1)


def _epi_v_transposed(acc, *, tk, dv, pad):
    ones = jnp.ones((pad, tk), F32)
    pieces = []
    for r in range(acc.shape[0] // tk):
        blk = acc[r * tk:(r + 1) * tk, :]
        rows = []
        for hh in range(acc.shape[1] // dv):
            rows += [blk[:, hh * dv:(hh + 1) * dv].T, ones]
        pieces.append(jnp.concatenate(rows, axis=0))
    return (_HeadRows(acc, dv), pieces)


def _epi_rope_q(acc, cos, sin, *, scale):
    return (_rope_tile(acc, cos, sin) * scale,)


def _epi_rope_k(acc, cos, sin):
    r = _rope_tile(acc, cos, sin)
    return (r, r)


def _epi_rope_k_heads(acc, cos, sin, *, dv):
    r = _rope_tile(acc, cos, sin)
    return (_HeadRows(r, dv), r)


def _epi_residual(acc, x):
    return (x + acc,)


def _epi_softplus_bias(acc, b):
    v = acc + b
    return (jnp.maximum(v, 0.0) + jnp.log(1.0 + jnp.exp(-jnp.abs(v))),)


def _epi_glu(acc, b, yg, g):
    ygf = yg.astype(F32)
    sig = 1.0 / (1.0 + jnp.exp(-(acc + b)))
    return (ygf * sig * _silu(g.astype(F32)),)


def _rope_table_kernel(inv_ref, cos_ref, sin_ref, *, tr, pos0, period):
    i = pl.program_id(0)
    row = lax.broadcasted_iota(jnp.int32, (tr, LANES), 0) + i * tr
    pos = (pos0 + row % period).astype(F32)
    ang = pos * inv_ref[...]
    lane = lax.broadcasted_iota(jnp.int32, (tr, LANES), 1)
    cos_ref[...] = jnp.cos(ang)
    sin_ref[...] = jnp.where(lane < LANES // 2, -jnp.sin(ang), jnp.sin(ang))


def _rope_tables(cfg, rows, pos0, period, name):
    dh = cfg.a_head_dim
    inv = 1.0 / (cfg.rope_theta ** (jnp.arange(0, dh, 2, dtype=F32) / dh))
    inv2 = jnp.concatenate([inv, inv]).reshape(1, dh)
    tr = _tile(rows, 512, 8)
    return pl.pallas_call(
        functools.partial(_rope_table_kernel, tr=tr, pos0=pos0, period=period),
        grid=(rows // tr,),
        in_specs=[pl.BlockSpec((1, dh), lambda i: (0, 0))],
        out_specs=[pl.BlockSpec((tr, dh), lambda i: (i, 0))] * 2,
        out_shape=[jax.ShapeDtypeStruct((rows, dh), F32)] * 2,
        compiler_params=_cp(("parallel",)),
        name=name,
    )(inv2)


def _lambda(lamv, lam_init):
    s1 = jnp.sum(lamv[0:1, :] * lamv[1:2, :], axis=-1, keepdims=True)
    s2 = jnp.sum(lamv[2:3, :] * lamv[3:4, :], axis=-1, keepdims=True)
    return jnp.exp(s1) - jnp.exp(s2) + lam_init


def _attn_finish(o, g, sw, lam_init, eps):
    ms = jnp.mean(o * o, axis=-1, keepdims=True)
    on = (o * lax.rsqrt(ms + eps) * sw) * (1.0 - lam_init)
    return (on * _silu(g.astype(F32))).astype(BF16)


def _attn_prompt_kernel(lamv_ref, q_ref, k_ref, vt_ref, g_ref, sw_ref, o_ref, m_sc, acc_sc, sa_sc, sb_sc,
                        *, t, dh, dv, chunk, lam_init, eps):
    qi = pl.program_id(1)
    q = q_ref[...]
    m_sc[...] = jnp.full(m_sc.shape, -jnp.inf, F32)
    acc_sc[...] = jnp.zeros(acc_sc.shape, F32)

    def scores(kb, s_sc):
        kblk = k_ref[pl.ds(pl.multiple_of(kb * t, t), t), :]
        for mp in range(2):
            s_sc[mp] = lax.dot_general(kblk[:, mp * dh:(mp + 1) * dh], q[:, mp * dh:(mp + 1) * dh],
                                       (((1,), (1,)), ((), ())), preferred_element_type=F32)

    def consume(kb, s_sc, masked):
        vt = vt_ref[kb]
        if masked:
            keys = lax.broadcasted_iota(jnp.int32, (t, t), 0)
            qs = lax.broadcasted_iota(jnp.int32, (t, t), 1)
            visible = (keys // chunk) <= (qs // chunk)
        for mp in range(2):
            st = s_sc[mp]
            if masked:
                st = jnp.where(visible, st, -jnp.inf)
            m_prev = m_sc[mp]
            m_new = jnp.maximum(m_prev, jnp.max(st, axis=0, keepdims=True))
            alpha = jnp.exp2(m_prev - m_new)
            p = jnp.exp2(st - m_new)
            acc_sc[mp] = alpha * acc_sc[mp] + jnp.dot(vt, p.astype(BF16), preferred_element_type=F32)
            m_sc[mp] = m_new

    nfull = qi
    scores(0, sa_sc)

    def pair(kb):
        consume(kb, sa_sc, False)
        scores(kb + 1, sb_sc)
        consume(kb + 1, sb_sc, False)
        scores(kb + 2, sa_sc)

    def quad_body(i, c):
        pair(4 * i)
        pair(4 * i + 2)
        return c

    def pair_body(i, c):
        pair(4 * (nfull // 4) + 2 * i)
        return c

    lax.fori_loop(0, nfull // 4, quad_body, 0)
    lax.fori_loop(0, (nfull % 4) // 2, pair_body, 0)

    @pl.when(nfull % 2 == 0)
    def _():
        consume(nfull, sa_sc, True)

    @pl.when(nfull % 2 == 1)
    def _():
        consume(nfull - 1, sa_sc, False)
        scores(nfull, sb_sc)
        consume(nfull, sb_sc, True)

    lam = _lambda(lamv_ref[...], lam_init)
    ot = (acc_sc[0, :dv, :] / acc_sc[0, dv:dv + 1, :]
          - lam * (acc_sc[1, :dv, :] / acc_sc[1, dv:dv + 1, :]))
    o_ref[...] = _attn_finish(ot.T, g_ref[...], sw_ref[...], lam_init, eps)


def _attn_prompt(cfg, q, k, vt3, g, lamv, sw, lam_init, name):
    L = q.shape[0]
    H, dv, dh = cfg.a_heads, cfg.a_vdim, cfg.a_head_dim
    tq = tk = vt3.shape[2]
    dva = vt3.shape[1] // H
    assert vt3.shape[0] * tk == L and tk % cfg.chunk == 0
    kern = functools.partial(_attn_prompt_kernel, t=tk, dh=dh, dv=dv, chunk=cfg.chunk,
                             lam_init=lam_init, eps=cfg.rms_eps)
    return pl.pallas_call(
        kern,
        grid=(H, L // tq),
        in_specs=[pl.BlockSpec((4, dh), lambda h, i: (0, 0)),
                  pl.BlockSpec((tq, dv), lambda h, i: (i, h)),
                  pl.BlockSpec((L, dv), lambda h, i: (0, h)),
                  pl.BlockSpec((L // tk, dva, tk), lambda h, i: (0, h, 0)),
                  pl.BlockSpec((tq, dv), lambda h, i: (i, h)),
                  pl.BlockSpec((1, dv), lambda h, i: (0, 0))],
        out_specs=pl.BlockSpec((tq, dv), lambda h, i: (i, h)),
        out_shape=jax.ShapeDtypeStruct((L, H * dv), BF16),
        scratch_shapes=[pltpu.VMEM((2, 1, tq), F32), pltpu.VMEM((2, dva, tq), F32),
                        pltpu.VMEM((2, tk, tq), F32), pltpu.VMEM((2, tk, tq), F32)],
        compiler_params=_cp(("parallel", "arbitrary")),
        name=name,
    )(lamv, q, k, vt3, g, sw)


def _attn_sample_kernel(lamv_ref, q_ref, kn_ref, vn_ref, kc_ref, vc_ref, g_ref, sw_ref, o_ref,
                        *, P, Ls, dh, hb, chunk, lam_init, eps):
    dv = 2 * dh
    qpos_p = P + lax.broadcasted_iota(jnp.int32, (Ls, P), 0)
    kpos_p = lax.broadcasted_iota(jnp.int32, (Ls, P), 1)
    vis_p = (kpos_p // chunk) <= (qpos_p // chunk)
    qpos_n = P + lax.broadcasted_iota(jnp.int32, (Ls, Ls), 0)
    kpos_n = P + lax.broadcasted_iota(jnp.int32, (Ls, Ls), 1)
    vis_n = (kpos_n // chunk) <= (qpos_n // chunk)
    lam = _lambda(lamv_ref[...], lam_init)
    dn = (((1,), (1,)), ((), ()))
    for hh in range(hb):
        c0 = hh * dv
        q = q_ref[:, c0:c0 + dv]
        kn = kn_ref[:, c0:c0 + dv]
        vn = vn_ref[:, c0:c0 + dv]
        kc = kc_ref[0, :, c0:c0 + dv].astype(BF16)
        vc = vc_ref[0, :, c0:c0 + dv].astype(BF16)
        outs = []
        for mp in range(2):
            qm = q[:, mp * dh:(mp + 1) * dh]
            sp = lax.dot_general(qm, kc[:, mp * dh:(mp + 1) * dh], dn, preferred_element_type=F32)
            sn = lax.dot_general(qm, kn[:, mp * dh:(mp + 1) * dh], dn, preferred_element_type=F32)
            sp = jnp.where(vis_p, sp, -jnp.inf)
            sn = jnp.where(vis_n, sn, -jnp.inf)
            m = jnp.maximum(jnp.max(sp, axis=-1, keepdims=True), jnp.max(sn, axis=-1, keepdims=True))
            pp = jnp.exp2(sp - m)
            pn = jnp.exp2(sn - m)
            l = jnp.sum(pp, axis=-1, keepdims=True) + jnp.sum(pn, axis=-1, keepdims=True)
            acc = (jnp.dot(pp.astype(BF16), vc, preferred_element_type=F32)
                   + jnp.dot(pn.astype(BF16), vn, preferred_element_type=F32))
            outs.append(acc / l)
        o_ref[:, c0:c0 + dv] = _attn_finish(outs[0] - lam * outs[1], g_ref[:, c0:c0 + dv], sw_ref[...],
                                            lam_init, eps)


def _attn_sample(cfg, q, k, v, g, kc, vc, lamv, sw, lam_init, name):
    B, Ls, P = cfg.dec_batch, cfg.dec_seq, cfg.past_len
    H, dv, dh = cfg.a_heads, cfg.a_vdim, cfg.a_head_dim
    hb = _tile(H, 4, 1)
    kern = functools.partial(_attn_sample_kernel, P=P, Ls=Ls, dh=dh, hb=hb, chunk=cfg.chunk,
                             lam_init=lam_init, eps=cfg.rms_eps)
    row = pl.BlockSpec((Ls, hb * dv), lambda b, h: (b, h))
    cache = pl.BlockSpec((1, P, hb * dv), lambda b, h: (b, 0, h))
    return pl.pallas_call(
        kern,
        grid=(B, H // hb),
        in_specs=[pl.BlockSpec((4, dh), lambda b, h: (0, 0)), row, row, row, cache, cache, row,
                  pl.BlockSpec((1, dv), lambda b, h: (0, 0))],
        out_specs=row,
        out_shape=jax.ShapeDtypeStruct((B * Ls, H * dv), BF16),
        compiler_params=_cp(("parallel", "parallel")),
        name=name,
    )(lamv, q, k, v, kc.reshape(B, P, H * dv), vc.reshape(B, P, H * dv), g, sw)


def _attn_layer(cfg, li, xp, xs, cache_k, cache_v, nw, w_in, lq1, lk1, lq2, lk2, subln, w_out, ropes):
    lam_init = 0.8 - 0.6 * math.exp(-0.3 * li)
    W = cfg.a_width
    w_in_b = w_in.astype(BF16)
    w_out_b = w_out.astype(BF16)
    lamv = jnp.stack([lq1, lk1, lq2, lk2]).astype(F32)
    sw = subln.reshape(1, cfg.a_vdim).astype(F32)
    scale = cfg.a_head_dim ** -0.5 * math.log2(math.e)
    res = []
    for tag, x, (cos, sin) in (("p", xp, ropes[0]), ("s", xs, ropes[1])):
        nm = f"l{li}{tag}"
        h = _rmsnorm(x, nw, BF16, cfg.rms_eps, nm + "_norm")
        tm, tn = cfg.mm_tm, cfg.mm_tn
        rope_ex = [(cos, "row"), (sin, "row")]
        (q,) = _matmul(h, w_in_b, 0, W, [BF16], functools.partial(_epi_rope_q, scale=scale), rope_ex,
                       tm, tn, nm + "_q")
        (g,) = _matmul(h, w_in_b, 3 * W, W, [BF16], _epi_plain, [], tm, tn, nm + "_g")
        if tag == "p":
            L = x.shape[0]
            H, dv, pad = cfg.a_heads, cfg.a_vdim, 16
            dva = dv + pad
            tk = _tile(L, cfg.attn_tk, LANES)
            hb = _tile(H, 8, 8)
            heads_out = (jax.ShapeDtypeStruct((L, H, dv), F32),
                         lambda bm, bn: ((bm, bn // dv, dv), lambda i, j: (i, j, 0)))
            vt_out = (jax.ShapeDtypeStruct((L // tk, H * dva, tk), BF16),
                      lambda bm, bn: ((bm // tk, bn // dv * dva, tk), lambda i, j: (i, j, 0)))
            k32, k16 = _matmul(h, w_in_b, W, W, [heads_out, BF16], functools.partial(_epi_rope_k_heads, dv=dv),
                               rope_ex, tk, hb * dv, nm + "_k", weights_outer=True)
            v32, vt3 = _matmul(h, w_in_b, 2 * W, W, [heads_out, vt_out],
                               functools.partial(_epi_v_transposed, tk=tk, dv=dv, pad=pad), [], tk, hb * dv,
                               nm + "_v", weights_outer=True)
            y = _attn_prompt(cfg, q, k16, vt3, g, lamv, sw, lam_init, nm + "_attn")
        else:
            k32, k16 = _matmul(h, w_in_b, W, W, [F32, BF16], _epi_rope_k, rope_ex, tm, tn, nm + "_k")
            v32, v16 = _matmul(h, w_in_b, 2 * W, W, [F32, BF16], _epi_two, [], tm, tn, nm + "_v")
            y = _attn_sample(cfg, q, k16, v16, g, cache_k, cache_v, lamv, sw, lam_init, nm + "_attn")
        (xn,) = _matmul(y, w_out_b, 0, cfg.d_model, [F32], _epi_residual, [(x, "tile")], tm, tn, nm + "_out")
        res.append((xn, k32, v32))
    return res


def _conv_kernel(x_ref, prev_ref, w_ref, b_ref, o_ref, carry_sc, *, tr, taps):
    r = pl.program_id(2)

    @pl.when(r == 0)
    def _():
        carry_sc[...] = prev_ref[0]

    x = x_ref[...].astype(F32)
    w = w_ref[...]
    b = b_ref[...]
    acc = b + x * w[taps - 1:taps, :]
    for j in range(1, taps):
        acc = acc + pltpu.roll(x, j, axis=0) * w[taps - 1 - j:taps - j, :]
    o_ref[...] = _silu(acc).astype(o_ref.dtype)
    x8 = x[0:SUBLANES, :]
    pr = carry_sc[...]
    row8 = lax.broadcasted_iota(jnp.int32, x8.shape, 0)
    acc8 = b + x8 * w[taps - 1:taps, :]
    for j in range(1, taps):
        sh = jnp.where(row8 < j, pltpu.roll(pr, j, axis=0), pltpu.roll(x8, j, axis=0))
        acc8 = acc8 + sh * w[taps - 1 - j:taps - j, :]
    o_ref[0:SUBLANES, :] = _silu(acc8).astype(o_ref.dtype)
    carry_sc[...] = x[tr - SUBLANES:tr, :]


def _conv_silu(xbc, prev8, w_t, b, nseq, name):
    R, C = xbc.shape
    taps = w_t.shape[0]
    Lseq = R // nseq
    tr = _tile(Lseq, 512, 16)
    tc = _tile(C, 512, LANES)
    nr = Lseq // tr
    return pl.pallas_call(
        functools.partial(_conv_kernel, tr=tr, taps=taps),
        grid=(C // tc, nseq, nr),
        in_specs=[pl.BlockSpec((tr, tc), lambda c, s, r: (s * nr + r, c)),
                  pl.BlockSpec((1, SUBLANES, tc), lambda c, s, r: (s, 0, c)),
                  pl.BlockSpec((taps, tc), lambda c, s, r: (0, c)),
                  pl.BlockSpec((1, tc), lambda c, s, r: (0, c))],
        out_specs=pl.BlockSpec((tr, tc), lambda c, s, r: (s * nr + r, c)),
        out_shape=jax.ShapeDtypeStruct((R, C), BF16),
        scratch_shapes=[pltpu.VMEM((SUBLANES, tc), F32)],
        compiler_params=_cp(("parallel", "arbitrary", "arbitrary")),
        name=name,
    )(xbc, prev8, w_t, b)


def _split_bf16(v):
    hi = v.astype(BF16)
    lo = (v - hi.astype(F32)).astype(BF16)
    return hi, lo


def _expand(v, e):
    hi, lo = _split_bf16(v)
    return (jnp.dot(hi, e, preferred_element_type=F32) + jnp.dot(lo, e, preferred_element_type=F32))


def _ssd_kernel(*refs, T, gw, hg, hd, N, has_init):
    if has_init:
        x_ref, b_ref, c_ref, dt_ref, alog_ref, d_ref, e_ref, init_ref, y_ref, st_ref, s_sc = refs
    else:
        x_ref, b_ref, c_ref, dt_ref, alog_ref, d_ref, e_ref, y_ref, st_ref, s_sc = refs
    ci = pl.program_id(2)

    @pl.when(ci == 0)
    def _():
        if has_init:
            s_sc[...] = init_ref[0].reshape(gw, N).T
        else:
            s_sc[...] = jnp.zeros(s_sc.shape, F32)

    dt = dt_ref[...]
    a_neg = -jnp.exp(alog_ref[...])
    a = dt * a_neg
    row = lax.broadcasted_iota(jnp.int32, a.shape, 0)
    d = 1
    while d < T:
        a = a + jnp.where(row >= d, pltpu.roll(a, d, axis=0), 0.0)
        d *= 2
    a_last = a[T - 1:T, :]
    e = e_ref[...]
    expa = jnp.exp(a)
    expa_x = jnp.dot(expa.astype(BF16), e, preferred_element_type=F32)
    dtdecay_x = jnp.dot((dt * jnp.exp(a_last - a)).astype(BF16), e, preferred_element_type=F32)
    exp_last_x = _expand(expa[T - SUBLANES:T, :], e)[SUBLANES - 1:SUBLANES, :]
    xb = x_ref[...]
    x = xb.astype(F32)
    bm = b_ref[...]
    cm = c_ref[...]
    cb = lax.dot_general(cm, bm, (((1,), (1,)), ((), ())), preferred_element_type=F32)
    s0 = s_sc[...]
    y_off = jnp.dot(cm, s0.astype(BF16), preferred_element_type=F32) * expa_x
    a2_t = (a - jnp.log(dt)).T
    tri = (lax.broadcasted_iota(jnp.int32, (T, T), 0) >= lax.broadcasted_iota(jnp.int32, (T, T), 1))
    lane = lax.broadcasted_iota(jnp.int32, (T, LANES), 1)
    per_pair = LANES // hd
    pieces = []
    for pi in range(gw // LANES):
        xp = xb[:, pi * LANES:(pi + 1) * LANES]
        piece = None
        for e_i in range(per_pair):
            hl = pi * per_pair + e_i
            seg = a[:, hl:hl + 1] - a2_t[hl:hl + 1, :]
            m = (cb * jnp.where(tri, jnp.exp(seg), 0.0)).astype(BF16)
            r = jnp.dot(m, xp, preferred_element_type=F32)
            if piece is None:
                piece = r
            else:
                piece = jnp.where(lane < e_i * hd, piece, r)
        pieces.append(piece)
    y_diag = pieces[0] if len(pieces) == 1 else jnp.concatenate(pieces, axis=1)
    y_ref[...] = (y_diag + y_off + d_ref[...] * x).astype(y_ref.dtype)
    xw = (x * dtdecay_x).astype(BF16)
    upd = lax.dot_general(bm, xw, (((0,), (0,)), ((), ())), preferred_element_type=F32)
    s_new = s0 * exp_last_x + upd
    s_sc[...] = s_new

    @pl.when(ci == pl.num_programs(2) - 1)
    def _():
        st_ref[0] = s_new.T.reshape(hg, hd, N)


def _ssd(cfg, xbc_act, dtg, alog_g, d_exp, e_mat, init, nseq, T, name):
    R = xbc_act.shape[0]
    G, N, hd = cfg.b_groups, cfg.b_state, cfg.b_head_dim
    hg = cfg.b_heads // G
    gw = hg * hd
    Lseq = R // nseq
    nc = Lseq // T
    bw = cfg.b_width
    boff = bw // N
    has_init = init is not None
    kern = functools.partial(_ssd_kernel, T=T, gw=gw, hg=hg, hd=hd, N=N, has_init=has_init)
    in_specs = [pl.BlockSpec((T, gw), lambda s, g, c: (s * nc + c, g)),
                pl.BlockSpec((T, N), lambda s, g, c: (s * nc + c, boff + g)),
                pl.BlockSpec((T, N), lambda s, g, c: (s * nc + c, boff + G + g)),
                pl.BlockSpec((T, LANES), lambda s, g, c: (s * nc + c, g)),
                pl.BlockSpec((1, LANES), lambda s, g, c: (0, g)),
                pl.BlockSpec((1, gw), lambda s, g, c: (0, g)),
                pl.BlockSpec((LANES, gw), lambda s, g, c: (0, 0))]
    args = [xbc_act, xbc_act, xbc_act, dtg, alog_g, d_exp, e_mat]
    if has_init:
        in_specs.append(pl.BlockSpec((1, hg, hd, N), lambda s, g, c: (s, g, 0, 0)))
        args.append(init)
    return pl.pallas_call(
        kern,
        grid=(nseq, G, nc),
        in_specs=in_specs,
        out_specs=[pl.BlockSpec((T, gw), lambda s, g, c: (s * nc + c, g)),
                   pl.BlockSpec((1, hg, hd, N), lambda s, g, c: (s, g, 0, 0))],
        out_shape=[jax.ShapeDtypeStruct((R, bw), BF16),
                   jax.ShapeDtypeStruct((nseq, cfg.b_heads, hd, N), F32)],
        scratch_shapes=[pltpu.VMEM((N, gw), F32)],
        compiler_params=_cp(("parallel", "parallel", "arbitrary")),
        name=name,
    )(*args)


def _ssd_layer(cfg, li, xp, xs, conv_prev, ssm_prev, nw, w_in, conv_w, conv_b, dt_bias, a_log, d_skip,
               gnorm, w_out):
    bw, cd, H, G = cfg.b_width, cfg.b_conv_dim, cfg.b_heads, cfg.b_groups
    hg = H // G
    hd = cfg.b_head_dim
    gw = hg * hd
    w_in_b = w_in.astype(BF16)
    w_out_b = w_out.astype(BF16)

    def per_group(v):
        lead = v.shape[:-1]
        v = v.reshape(lead + (G, hg))
        v = jnp.pad(v, [(0, 0)] * len(lead) + [(0, 0), (0, LANES - hg)])
        return v.reshape(lead + (G * LANES,))

    w_dt_g = per_group(w_in[:, bw + cd:]).astype(BF16)
    dtb_g = per_group(dt_bias.astype(F32)).reshape(1, G * LANES)
    alog_g = per_group(a_log.astype(F32)).reshape(1, G * LANES)
    d_exp = jnp.repeat(d_skip.astype(F32), hd).reshape(1, bw)
    e_mat = (jnp.arange(LANES)[:, None] == (jnp.arange(gw)[None, :] // hd)).astype(BF16)
    w_t = conv_w.astype(F32).T
    cb2 = conv_b.astype(F32).reshape(1, cd)
    T = cfg.ssd_chunk
    B, Ls = cfg.dec_batch, cfg.dec_seq
    res = []
    for tag, x in (("p", xp), ("s", xs)):
        nm = f"l{li}{tag}"
        tm, tn = cfg.mm_tm, cfg.mm_tn
        h = _rmsnorm(x, nw, BF16, cfg.rms_eps, nm + "_norm")
        (z,) = _matmul(h, w_in_b, 0, bw, [BF16], _epi_plain, [], tm, tn, nm + "_z")
        (xbc,) = _matmul(h, w_in_b, bw, cd, [BF16], _epi_plain, [], tm, tn, nm + "_xbc")
        (dtg,) = _matmul(h, w_dt_g, 0, G * LANES, [F32], _epi_softplus_bias, [(dtb_g, "col")], tm, tn,
                         nm + "_dt")
        if tag == "p":
            prev8 = jnp.zeros((1, SUBLANES, cd), F32)
            act = _conv_silu(xbc, prev8, w_t, cb2, 1, nm + "_conv")
            conv_new = xbc[x.shape[0] - (cfg.b_conv - 1):].astype(F32)[None]
            y, st = _ssd(cfg, act, dtg, alog_g, d_exp, e_mat, None, 1, T, nm + "_ssd")
        else:
            prev8 = jnp.pad(conv_prev.astype(F32), ((0, 0), (SUBLANES - (cfg.b_conv - 1), 0), (0, 0)))
            act = _conv_silu(xbc, prev8, w_t, cb2, B, nm + "_conv")
            conv_new = xbc.reshape(B, Ls, cd)[:, Ls - (cfg.b_conv - 1):].astype(F32)
            act_p = jnp.pad(act.reshape(B, Ls, cd), ((0, 0), (0, T - Ls), (0, 0))).reshape(B * T, cd)
            dt_p = jnp.pad(dtg.reshape(B, Ls, G * LANES), ((0, 0), (0, T - Ls), (0, 0))).reshape(B * T, G * LANES)
            y_p, st = _ssd(cfg, act_p, dt_p, alog_g, d_exp, e_mat, ssm_prev.astype(F32), B, T, nm + "_ssd")
            y = y_p.reshape(B, T, bw)[:, :Ls].reshape(B * Ls, bw)
        yn = _gated_rmsnorm(y, z, gnorm, cfg.rms_eps, nm + "_gnorm")
        (xn,) = _matmul(yn, w_out_b, 0, cfg.d_model, [F32], _epi_residual, [(x, "tile")], tm // 2, tn,
                        nm + "_out")
        res.append((xn, conv_new, st))
    return res


def _s5_disc_kernel(lr_ref, li_ref, ls_ref, br_ref, bi_ref, ar_ref, ai_ref, bbr_ref, bbi_ref):
    lr = lr_ref[...]
    li = li_ref[...]
    step = jnp.exp(ls_ref[...])
    mag = jnp.exp(lr * step)
    ar = mag * jnp.cos(li * step)
    ai = mag * jnp.sin(li * step)
    den = lr * lr + li * li
    nr = ar - 1.0
    ni = ai
    cr = ((nr * lr + ni * li) / den)[:, None, :]
    ci = ((ni * lr - nr * li) / den)[:, None, :]
    br = br_ref[...]
    bi = bi_ref[...]
    ar_ref[...] = ar
    ai_ref[...] = ai
    bbr_ref[...] = cr * br - ci * bi
    bbi_ref[...] = cr * bi + ci * br


def _s5_discretize(cfg, lam_re, lam_im, log_step, b_re, b_im, name):
    G, P, K = cfg.c_groups, cfg.c_state, cfg.c_group
    bt_r = jnp.transpose(b_re.astype(F32), (0, 2, 1))
    bt_i = jnp.transpose(b_im.astype(F32), (0, 2, 1))
    tg = _tile(G, 64, 8)
    s2 = pl.BlockSpec((tg, P), lambda i: (i, 0))
    s3 = pl.BlockSpec((tg, K, P), lambda i: (i, 0, 0))
    return pl.pallas_call(
        _s5_disc_kernel,
        grid=(G // tg,),
        in_specs=[s2, s2, pl.BlockSpec((tg, 1), lambda i: (i, 0)), s3, s3],
        out_specs=[s2, s2, s3, s3],
        out_shape=[jax.ShapeDtypeStruct((G, P), F32)] * 2 + [jax.ShapeDtypeStruct((G, K, P), F32)] * 2,
        compiler_params=_cp(("parallel",)),
        name=name,
    )(lam_re.astype(F32), lam_im.astype(F32), log_step.astype(F32).reshape(G, 1), bt_r, bt_i)


def _gelu_tanh(v):
    return 0.5 * v * (1.0 + jnp.tanh(math.sqrt(2.0 / math.pi) * (v + 0.044715 * (v * v * v))))


def _s5_scan_kernel(*refs, tT, pitch, nsub, sw, emit_y):
    if emit_y:
        (u_ref, bbd_ref, cbd_ref, ar_ref, ai_ref, d_ref, hr0_ref, hi0_ref,
         yg_ref, hrT_ref, hiT_ref, bu_a, bu_b, hs_a, hs_b, hr_sc, hi_sc) = refs
    else:
        (u_ref, bbd_ref, ar_ref, ai_ref, hr0_ref, hi0_ref,
         hrT_ref, hiT_ref, bu_a, bu_b, hr_sc, hi_sc) = refs
        hs_a = hs_b = None
    tb = pl.program_id(2)

    @pl.when(tb == 0)
    def _():
        hr_sc[...] = hr0_ref[...]
        hi_sc[...] = hi0_ref[...]

    nl = sw // LANES
    ar = jnp.broadcast_to(ar_ref[...], (SUBLANES, sw))
    ai = jnp.broadcast_to(ai_ref[...], (SUBLANES, sw))
    hr = hr_sc[...]
    hi = hi_sc[...]
    ts = tT // nsub
    bufs = ((bu_a, hs_a), (bu_b, hs_b))

    def in_map(sb):
        bu_sc = bufs[sb % 2][0]
        ub = u_ref[:, sb * ts:(sb + 1) * ts, :].reshape(SUBLANES * ts, LANES)
        bu = jnp.dot(ub, bbd_ref[0], preferred_element_type=F32)
        for l in range(2 * nl):
            for sq in range(SUBLANES):
                bu_sc[l, sq * pitch:sq * pitch + ts, :] = bu[sq * ts:(sq + 1) * ts, l * LANES:(l + 1) * LANES]

    def out_map(sb):
        hs_sc = bufs[sb % 2][1]
        ub = u_ref[:, sb * ts:(sb + 1) * ts, :].reshape(SUBLANES * ts, LANES)
        hs = jnp.concatenate(
            [jnp.concatenate([hs_sc[l, sq * pitch:sq * pitch + ts, :] for sq in range(SUBLANES)], axis=0)
             for l in range(2 * nl)], axis=1).astype(BF16)
        y = jnp.dot(hs, cbd_ref[0], preferred_element_type=F32)
        v = y + d_ref[...] * ub.astype(F32)
        yg_ref[:, sb * ts:(sb + 1) * ts, :] = _gelu_tanh(v).reshape(SUBLANES, ts, LANES).astype(yg_ref.dtype)

    in_map(0)
    for sb in range(nsub):
        bu_sc, hs_sc = bufs[sb % 2]
        if sb + 1 < nsub:
            in_map(sb + 1)
        if emit_y and sb >= 1:
            out_map(sb - 1)
        for t in range(ts):
            rows = pl.ds(t, SUBLANES, stride=pitch)
            bur = jnp.concatenate([bu_sc[l, rows, :] for l in range(nl)], axis=1)
            bui = jnp.concatenate([bu_sc[nl + l, rows, :] for l in range(nl)], axis=1)
            hr, hi = ar * hr - ai * hi + bur, ar * hi + ai * hr + bui
            if emit_y:
                for l in range(nl):
                    hs_sc[l, rows, :] = hr[:, l * LANES:(l + 1) * LANES]
                    hs_sc[nl + l, rows, :] = hi[:, l * LANES:(l + 1) * LANES]
    if emit_y:
        out_map(nsub - 1)
    hr_sc[...] = hr
    hi_sc[...] = hi

    @pl.when(tb == pl.num_programs(2) - 1)
    def _():
        hrT_ref[...] = hr
        hiT_ref[...] = hi


def _s5_scan(cfg, u3, bbd, cbd, ar, ai, d2, hr0, hi0, emit_y, name):
    NS, T, C = u3.shape
    nj = C // LANES
    sw = bbd.shape[2] // 2
    tT = _tile(T, cfg.s5_tt, 16)
    nsg = NS // SUBLANES
    nsub = max(1, tT // 16)
    ts = tT // nsub
    pitch = ts + SUBLANES if (ts // SUBLANES) % 2 == 0 else ts
    kern = functools.partial(_s5_scan_kernel, tT=tT, pitch=pitch, nsub=nsub, sw=sw, emit_y=emit_y)
    u_spec = pl.BlockSpec((SUBLANES, tT, LANES), lambda j, s, t: (s, t, j))
    bbd_spec = pl.BlockSpec((1, LANES, 2 * sw), lambda j, s, t: (j, 0, 0))
    a_spec = pl.BlockSpec((1, sw), lambda j, s, t: (0, j))
    st_spec = pl.BlockSpec((SUBLANES, sw), lambda j, s, t: (s, j))
    st_shape = jax.ShapeDtypeStruct((NS, nj * sw), F32)
    slab = pltpu.VMEM((2 * sw // LANES, SUBLANES * pitch, LANES), F32)
    scratch = [slab, slab]
    if emit_y:
        in_specs = [u_spec, bbd_spec, pl.BlockSpec((1, 2 * sw, LANES), lambda j, s, t: (j, 0, 0)),
                    a_spec, a_spec, pl.BlockSpec((1, LANES), lambda j, s, t: (0, j)), st_spec, st_spec]
        args = [u3, bbd, cbd, ar, ai, d2, hr0, hi0]
        out_specs = [u_spec, st_spec, st_spec]
        out_shape = [jax.ShapeDtypeStruct((NS, T, C), BF16), st_shape, st_shape]
        scratch += [slab, slab]
    else:
        in_specs = [u_spec, bbd_spec, a_spec, a_spec, st_spec, st_spec]
        args = [u3, bbd, ar, ai, hr0, hi0]
        out_specs = [st_spec, st_spec]
        out_shape = [st_shape, st_shape]
    scratch += [pltpu.VMEM((SUBLANES, sw), F32)] * 2
    return pl.pallas_call(
        kern,
        grid=(nj, nsg, T // tT),
        in_specs=in_specs, out_specs=out_specs, out_shape=out_shape,
        scratch_shapes=scratch,
        compiler_params=_cp(("parallel", "parallel", "arbitrary")),
        name=name,
    )(*args)


def _s5_combine_kernel(er_ref, ei_ref, ar_ref, ai_ref, h0r_ref, h0i_ref, hr_ref, hi_ref, *, nseg, nsq):
    pr = ar_ref[...]
    pi_ = ai_ref[...]
    for _ in range(nsq):
        pr, pi_ = pr * pr - pi_ * pi_, 2.0 * pr * pi_
    hr = h0r_ref[...]
    hi = h0i_ref[...]
    hr_ref[0:1, :] = hr
    hi_ref[0:1, :] = hi
    for i in range(1, nseg):
        er = er_ref[i - 1:i, :]
        ei = ei_ref[i - 1:i, :]
        hr, hi = pr * hr - pi_ * hi + er, pr * hi + pi_ * hr + ei
        hr_ref[i:i + 1, :] = hr
        hi_ref[i:i + 1, :] = hi


def _s5_combine(er, ei, ar, ai, h0r, h0i, seg_len, name):
    nseg, W = er.shape
    nsq = int(round(math.log2(seg_len)))
    assert 2 ** nsq == seg_len
    tw = _tile(W, 1024, LANES)
    full = pl.BlockSpec((nseg, tw), lambda i: (0, i))
    row = pl.BlockSpec((1, tw), lambda i: (0, i))
    return pl.pallas_call(
        functools.partial(_s5_combine_kernel, nseg=nseg, nsq=nsq),
        grid=(W // tw,),
        in_specs=[full, full, row, row, row, row],
        out_specs=[full, full],
        out_shape=[jax.ShapeDtypeStruct((nseg, W), F32)] * 2,
        compiler_params=_cp(("parallel",)),
        name=name,
    )(er, ei, ar, ai, h0r, h0i)


def _s5_layer(cfg, li, xp, xs, s_re, s_im, nw, w_in, lam_re, lam_im, log_step, b_re, b_im, c_re, c_im,
              d_skip, w_glu, b_glu, w_out):
    CW, G, P, K = cfg.c_width, cfg.c_groups, cfg.c_state, cfg.c_group
    gpb = LANES // K
    nj = CW // LANES
    sw = gpb * P
    w_in_b = w_in.astype(BF16)
    w_glu_b = w_glu.astype(BF16)
    w_out_b = w_out.astype(BF16)
    ar, ai, bbr, bbi = _s5_discretize(cfg, lam_re, lam_im, log_step, b_re, b_im, f"l{li}_disc")
    eye = jnp.eye(gpb, dtype=F32)
    bb = jnp.stack([bbr, bbi]).reshape(2, nj, gpb, K, P)
    bbd = jnp.einsum("rjgkp,gh->jgkrhp", bb, eye).reshape(nj, LANES, 2 * sw).astype(BF16)
    cc = jnp.stack([c_re.astype(F32), -c_im.astype(F32)]).reshape(2, nj, gpb, K, P)
    cbd = jnp.einsum("rjgkp,gh->jrgphk", cc, eye).reshape(nj, 2 * sw, LANES).astype(BF16)
    ar2 = ar.reshape(1, G * P)
    ai2 = ai.reshape(1, G * P)
    d2 = d_skip.astype(F32).reshape(1, CW)
    bg2 = b_glu.astype(F32).reshape(1, CW)
    B, Ls = cfg.dec_batch, cfg.dec_seq
    res = []
    for tag, x in (("p", xp), ("s", xs)):
        nm = f"l{li}{tag}"
        tm, tn = cfg.mm_tm, cfg.mm_tn
        h = _rmsnorm(x, nw, BF16, cfg.rms_eps, nm + "_norm")
        (u,) = _matmul(h, w_in_b, 0, CW, [BF16], _epi_plain, [], tm, tn, nm + "_u")
        (g,) = _matmul(h, w_in_b, CW, CW, [BF16], _epi_plain, [], tm, tn, nm + "_g")
        if tag == "p":
            L = x.shape[0]
            seg = _tile(L // SUBLANES, cfg.s5_seg, 16)
            nseg = L // seg
            u3 = u.reshape(nseg, seg, CW)
            zs = jnp.zeros((nseg, G * P), F32)
            er, ei = _s5_scan(cfg, u3, bbd, None, ar2, ai2, None, zs, zs, False, nm + "_scan_a")
            z1 = jnp.zeros((1, G * P), F32)
            h0r, h0i = _s5_combine(er, ei, ar2, ai2, z1, z1, seg, nm + "_comb")
            yg3, hrT, hiT = _s5_scan(cfg, u3, bbd, cbd, ar2, ai2, d2, h0r, h0i, True, nm + "_scan_b")
            yg = yg3.reshape(L, CW)
            st_r = hrT[nseg - 1:nseg].reshape(1, G, P)
            st_i = hiT[nseg - 1:nseg].reshape(1, G, P)
        else:
            u3 = u.reshape(B, Ls, CW)
            yg3, hrT, hiT = _s5_scan(cfg, u3, bbd, cbd, ar2, ai2, d2,
                                     s_re.astype(F32).reshape(B, G * P), s_im.astype(F32).reshape(B, G * P),
                                     True, nm + "_scan")
            yg = yg3.reshape(B * Ls, CW)
            st_r = hrT.reshape(B, G, P)
            st_i = hiT.reshape(B, G, P)
        (y2,) = _matmul(yg, w_glu_b, 0, CW, [BF16], _epi_glu, [(bg2, "col"), (yg, "tile"), (g, "tile")],
                        tm, tn, nm + "_glu")
        (xn,) = _matmul(y2, w_out_b, 0, cfg.d_model, [F32], _epi_residual, [(x, "tile")], tm, tn, nm + "_out")
        res.append((xn, st_r, st_i))
    return res


def _forward(cfg, x_prompt, x_sample,
             cache_k_l0, cache_v_l0, state_conv_l1, state_ssm_l1,
             state_s5_re_l2, state_s5_im_l2, cache_k_l3, cache_v_l3,
             norm_l0, attn_w_in_l0, lam_q1_l0, lam_k1_l0, lam_q2_l0, lam_k2_l0, subln_l0, attn_w_out_l0,
             norm_l1, ssm_w_in_l1, conv_w_l1, conv_b_l1, dt_bias_l1, a_log_l1, d_l1, gnorm_l1, ssm_w_out_l1,
             norm_l2, s5_w_in_l2, lam_re_l2, lam_im_l2, log_step_l2, b_re_l2, b_im_l2, c_re_l2, c_im_l2,
             d_l2, glu_w_l2, glu_b_l2, s5_w_out_l2,
             norm_l3, attn_w_in_l3, lam_q1_l3, lam_k1_l3, lam_q2_l3, lam_k2_l3, subln_l3, attn_w_out_l3,
             final_norm):
    D = cfg.d_model
    L, B, Ls, P = cfg.seq, cfg.dec_batch, cfg.dec_seq, cfg.past_len
    H, dv = cfg.a_heads, cfg.a_vdim
    xp = x_prompt.reshape(L, D).astype(F32)
    xs = x_sample.reshape(B * Ls, D).astype(F32)
    ropes = (_rope_tables(cfg, L, 0, L, "rope_p"), _rope_tables(cfg, B * Ls, P, Ls, "rope_s"))

    (xp, kp0, vp0), (xs, ks0, vs0) = _attn_layer(
        cfg, 0, xp, xs, cache_k_l0, cache_v_l0, norm_l0, attn_w_in_l0, lam_q1_l0, lam_k1_l0, lam_q2_l0,
        lam_k2_l0, subln_l0, attn_w_out_l0, ropes)
    (xp, conv_p1, ssm_p1), (xs, conv_s1, ssm_s1) = _ssd_layer(
        cfg, 1, xp, xs, state_conv_l1, state_ssm_l1, norm_l1, ssm_w_in_l1, conv_w_l1, conv_b_l1, dt_bias_l1,
        a_log_l1, d_l1, gnorm_l1, ssm_w_out_l1)
    (xp, s5re_p2, s5im_p2), (xs, s5re_s2, s5im_s2) = _s5_layer(
        cfg, 2, xp, xs, state_s5_re_l2, state_s5_im_l2, norm_l2, s5_w_in_l2, lam_re_l2, lam_im_l2,
        log_step_l2, b_re_l2, b_im_l2, c_re_l2, c_im_l2, d_l2, glu_w_l2, glu_b_l2, s5_w_out_l2)
    (xp, kp3, vp3), (xs, ks3, vs3) = _attn_layer(
        cfg, 3, xp, xs, cache_k_l3, cache_v_l3, norm_l3, attn_w_in_l3, lam_q1_l3, lam_k1_l3, lam_q2_l3,
        lam_k2_l3, subln_l3, attn_w_out_l3, ropes)

    y_prompt = _rmsnorm(xp, final_norm, F32, cfg.rms_eps, "final_p").reshape(1, L, D)
    y_sample = _rmsnorm(xs, final_norm, F32, cfg.rms_eps, "final_s").reshape(B, Ls, D)

    def kv_p(a):
        return a.reshape(1, L, H, dv)

    def kv_s(a):
        return a.reshape(B, Ls, H, dv)

    return (y_prompt, y_sample, kv_p(kp0), kv_p(vp0), kv_s(ks0), kv_s(vs0),
            conv_p1, ssm_p1, conv_s1, ssm_s1, s5re_p2, s5im_p2, s5re_s2, s5im_s2,
            kv_p(kp3), kv_p(vp3), kv_s(ks3), kv_s(vs3))


_CFG = Cfg()


def kernel(x_prompt, x_sample, cache_k_l0, cache_v_l0, state_conv_l1, state_ssm_l1, state_s5_re_l2, state_s5_im_l2, cache_k_l3, cache_v_l3, norm_l0, attn_w_in_l0, lam_q1_l0, lam_k1_l0, lam_q2_l0, lam_k2_l0, subln_l0, attn_w_out_l0, norm_l1, ssm_w_in_l1, conv_w_l1, conv_b_l1, dt_bias_l1, a_log_l1, d_l1, gnorm_l1, ssm_w_out_l1, norm_l2, s5_w_in_l2, lam_re_l2, lam_im_l2, log_step_l2, b_re_l2, b_im_l2, c_re_l2, c_im_l2, d_l2, glu_w_l2, glu_b_l2, s5_w_out_l2, norm_l3, attn_w_in_l3, lam_q1_l3, lam_k1_l3, lam_q2_l3, lam_k2_l3, subln_l3, attn_w_out_l3, final_norm):
    return _forward(_CFG, x_prompt, x_sample, cache_k_l0, cache_v_l0, state_conv_l1, state_ssm_l1, state_s5_re_l2, state_s5_im_l2, cache_k_l3, cache_v_l3, norm_l0, attn_w_in_l0, lam_q1_l0, lam_k1_l0, lam_q2_l0, lam_k2_l0, subln_l0, attn_w_out_l0, norm_l1, ssm_w_in_l1, conv_w_l1, conv_b_l1, dt_bias_l1, a_log_l1, d_l1, gnorm_l1, ssm_w_out_l1, norm_l2, s5_w_in_l2, lam_re_l2, lam_im_l2, log_step_l2, b_re_l2, b_im_l2, c_re_l2, c_im_l2, d_l2, glu_w_l2, glu_b_l2, s5_w_out_l2, norm_l3, attn_w_in_l3, lam_q1_l3, lam_k1_l3, lam_q2_l3, lam_k2_l3, subln_l3, attn_w_out_l3, final_norm)
```

```python
import dataclasses
import functools
import math

import jax
import jax.numpy as jnp
from jax import lax
from jax.experimental import pallas as pl
from jax.experimental.pallas import tpu as pltpu

F32 = jnp.float32
BF16 = jnp.bfloat16

LANES = 128
SUBLANES = 8
VMEM_LIMIT_BYTES = 48 * 1024 * 1024


@dataclasses.dataclass(frozen=True)
class Cfg:
    d_model: int = 4096
    seq: int = 16384
    dec_batch: int = 16
    dec_seq: int = 16
    past_len: int = 1024
    chunk: int = 64
    rms_eps: float = 1e-6
    rope_theta: float = 10000.0
    a_heads: int = 16
    a_head_dim: int = 128
    b_head_dim: int = 64
    b_groups: int = 8
    b_state: int = 128
    b_conv: int = 4
    c_group: int = 16
    c_state: int = 64
    attn_tk: int = 512
    ssd_chunk: int = 128
    s5_seg: int = 256
    s5_tt: int = 128
    mm_tm: int = 1024
    mm_tn: int = 512

    @property
    def a_vdim(self):
        return 2 * self.a_head_dim

    @property
    def a_width(self):
        return self.a_heads * self.a_vdim

    @property
    def b_width(self):
        return 2 * self.d_model

    @property
    def b_heads(self):
        return self.b_width // self.b_head_dim

    @property
    def b_conv_dim(self):
        return self.b_width + 2 * self.b_groups * self.b_state

    @property
    def c_width(self):
        return self.d_model

    @property
    def c_groups(self):
        return self.c_width // self.c_group


def _cp(sem):
    return pltpu.CompilerParams(dimension_semantics=sem, vmem_limit_bytes=VMEM_LIMIT_BYTES)


def _tile(n, pref, mult):
    if n <= pref:
        return n
    t = (pref // mult) * mult
    while t > mult and n % t:
        t -= mult
    assert n % t == 0, (n, pref, mult)
    return t


def _silu(x):
    return x / (1.0 + jnp.exp(-x))


def _rmsnorm_kernel(x_ref, w_ref, o_ref, *, eps):
    x = x_ref[...].astype(F32)
    ms = jnp.mean(x * x, axis=-1, keepdims=True)
    o_ref[...] = (x * lax.rsqrt(ms + eps) * w_ref[...]).astype(o_ref.dtype)


def _rmsnorm(x, w, out_dtype, eps, name):
    R, D = x.shape
    tr = _tile(R, 256, 8)
    return pl.pallas_call(
        functools.partial(_rmsnorm_kernel, eps=eps),
        grid=(R // tr,),
        in_specs=[pl.BlockSpec((tr, D), lambda i: (i, 0)), pl.BlockSpec((1, D), lambda i: (0, 0))],
        out_specs=pl.BlockSpec((tr, D), lambda i: (i, 0)),
        out_shape=jax.ShapeDtypeStruct((R, D), out_dtype),
        compiler_params=_cp(("parallel",)),
        name=name,
    )(x, w.reshape(1, D).astype(F32))


def _gated_rmsnorm_kernel(y_ref, z_ref, w_ref, o_ref, *, eps):
    y = y_ref[...].astype(F32) * _silu(z_ref[...].astype(F32))
    ms = jnp.mean(y * y, axis=-1, keepdims=True)
    o_ref[...] = (y * lax.rsqrt(ms + eps) * w_ref[...]).astype(o_ref.dtype)


def _gated_rmsnorm(y, z, w, eps, name):
    R, D = y.shape
    tr = _tile(R, 128, 8)
    return pl.pallas_call(
        functools.partial(_gated_rmsnorm_kernel, eps=eps),
        grid=(R // tr,),
        in_specs=[pl.BlockSpec((tr, D), lambda i: (i, 0)), pl.BlockSpec((tr, D), lambda i: (i, 0)),
                  pl.BlockSpec((1, D), lambda i: (0, 0))],
        out_specs=pl.BlockSpec((tr, D), lambda i: (i, 0)),
        out_shape=jax.ShapeDtypeStruct((R, D), BF16),
        compiler_params=_cp(("parallel",)),
        name=name,
    )(y, z, w.reshape(1, D).astype(F32))


@dataclasses.dataclass
class _HeadRows:
    tile: jax.Array
    dv: int


def _mm_kernel(*refs, n_extra, epi):
    a_ref, w_ref = refs[0], refs[1]
    extras = refs[2:2 + n_extra]
    outs = refs[2 + n_extra:]
    acc = jnp.dot(a_ref[...], w_ref[...], preferred_element_type=F32)
    res = epi(acc, *[e[...] for e in extras])
    for o, r in zip(outs, res):
        if isinstance(r, _HeadRows):
            for hh in range(r.tile.shape[1] // r.dv):
                o[:, hh, :] = r.tile[:, hh * r.dv:(hh + 1) * r.dv].astype(o.dtype)
        elif isinstance(r, (list, tuple)):
            for idx, piece in enumerate(r):
                o[idx] = piece.astype(o.dtype)
        else:
            o[...] = r.astype(o.dtype)


def _matmul(a, w, col0, ncols, out_dtypes, epi, extras, tm, tn, name, weights_outer=False):
    R, K = a.shape
    tm = _tile(R, tm, 8)
    tn = _tile(ncols, tn, LANES)
    assert col0 % tn == 0
    cb0 = col0 // tn

    def ix(f):
        return (lambda jj, ii: f(ii, jj)) if weights_outer else f

    w_kw = dict(pipeline_mode=pl.Buffered(1)) if weights_outer else {}
    in_specs = [pl.BlockSpec((tm, K), ix(lambda i, j: (i, 0))),
                pl.BlockSpec((K, tn), ix(lambda i, j: (0, cb0 + j)), **w_kw)]
    args = [a, w]
    for arr, kind in extras:
        if kind == "row":
            in_specs.append(pl.BlockSpec((tm, arr.shape[1]), ix(lambda i, j: (i, 0))))
        elif kind == "tile":
            in_specs.append(pl.BlockSpec((tm, tn), ix(lambda i, j: (i, j))))
        elif kind == "col":
            in_specs.append(pl.BlockSpec((1, tn), ix(lambda i, j: (0, j))))
        else:
            raise ValueError(kind)
        args.append(arr)
    out_specs, out_shape = [], []
    for dt in out_dtypes:
        if isinstance(dt, tuple):
            block, fn = dt[1](tm, tn)
            out_shape.append(dt[0])
            out_specs.append(pl.BlockSpec(block, ix(fn)))
        else:
            out_shape.append(jax.ShapeDtypeStruct((R, ncols), dt))
            out_specs.append(pl.BlockSpec((tm, tn), ix(lambda i, j: (i, j))))
    grid = (ncols // tn, R // tm) if weights_outer else (R // tm, ncols // tn)
    return pl.pallas_call(
        functools.partial(_mm_kernel, n_extra=len(extras), epi=epi),
        grid=grid,
        in_specs=in_specs, out_specs=out_specs, out_shape=out_shape,
        compiler_params=_cp(("arbitrary", "arbitrary") if weights_outer else ("parallel", "arbitrary")),
        name=name,
    )(*args)


def _epi_plain(acc):
    return (acc,)


def _epi_two(acc):
    return (acc, acc)


def _rope_tile(acc, cos, sin):
    parts = []
    for c in range(acc.shape[1] // LANES):
        x = acc[:, c * LANES:(c + 1) * LANES]
        parts.append(x * cos + pltpu.roll(x, LANES // 2, axis=1) * sin)
    return parts[0] if len(parts) == 1 else jnp.concatenate(parts, axis=1)


def _epi_v_transposed(acc, *, tk, dv, pad):
    ones = jnp.ones((pad, tk), F32)
    pieces = []
    for r in range(acc.shape[0] // tk):
        blk = acc[r * tk:(r + 1) * tk, :]
        rows = []
        for hh in range(acc.shape[1] // dv):
            rows += [blk[:, hh * dv:(hh + 1) * dv].T, ones]
        pieces.append(jnp.concatenate(rows, axis=0))
    return (_HeadRows(acc, dv), pieces)


def _epi_rope_q(acc, cos, sin, *, scale):
    return (_rope_tile(acc, cos, sin) * scale,)


def _epi_rope_k(acc, cos, sin):
    r = _rope_tile(acc, cos, sin)
    return (r, r)


def _epi_rope_k_heads(acc, cos, sin, *, dv):
    r = _rope_tile(acc, cos, sin)
    return (_HeadRows(r, dv), r)


def _epi_residual(acc, x):
    return (x + acc,)


def _epi_softplus_bias(acc, b):
    v = acc + b
    return (jnp.maximum(v, 0.0) + jnp.log(1.0 + jnp.exp(-jnp.abs(v))),)


def _epi_glu(acc, b, yg, g):
    ygf = yg.astype(F32)
    sig = 1.0 / (1.0 + jnp.exp(-(acc + b)))
    return (ygf * sig * _silu(g.astype(F32)),)


def _rope_table_kernel(inv_ref, cos_ref, sin_ref, *, tr, pos0, period):
    i = pl.program_id(0)
    row = lax.broadcasted_iota(jnp.int32, (tr, LANES), 0) + i * tr
    pos = (pos0 + row % period).astype(F32)
    ang = pos * inv_ref[...]
    lane = lax.broadcasted_iota(jnp.int32, (tr, LANES), 1)
    cos_ref[...] = jnp.cos(ang)
    sin_ref[...] = jnp.where(lane < LANES // 2, -jnp.sin(ang), jnp.sin(ang))


def _rope_tables(cfg, rows, pos0, period, name):
    dh = cfg.a_head_dim
    inv = 1.0 / (cfg.rope_theta ** (jnp.arange(0, dh, 2, dtype=F32) / dh))
    inv2 = jnp.concatenate([inv, inv]).reshape(1, dh)
    tr = _tile(rows, 512, 8)
    return pl.pallas_call(
        functools.partial(_rope_table_kernel, tr=tr, pos0=pos0, period=period),
        grid=(rows // tr,),
        in_specs=[pl.BlockSpec((1, dh), lambda i: (0, 0))],
        out_specs=[pl.BlockSpec((tr, dh), lambda i: (i, 0))] * 2,
        out_shape=[jax.ShapeDtypeStruct((rows, dh), F32)] * 2,
        compiler_params=_cp(("parallel",)),
        name=name,
    )(inv2)


def _lambda(lamv, lam_init):
    s1 = jnp.sum(lamv[0:1, :] * lamv[1:2, :], axis=-1, keepdims=True)
    s2 = jnp.sum(lamv[2:3, :] * lamv[3:4, :], axis=-1, keepdims=True)
    return jnp.exp(s1) - jnp.exp(s2) + lam_init


def _attn_finish(o, g, sw, lam_init, eps):
    ms = jnp.mean(o * o, axis=-1, keepdims=True)
    on = (o * lax.rsqrt(ms + eps) * sw) * (1.0 - lam_init)
    return (on * _silu(g.astype(F32))).astype(BF16)


def _attn_prompt_kernel(lamv_ref, q_ref, k_ref, vt_ref, g_ref, sw_ref, o_ref, m_sc, acc_sc, sa_sc, sb_sc,
                        *, t, dh, dv, chunk, lam_init, eps):
    qi = pl.program_id(1)
    q = q_ref[...]
    m_sc[...] = jnp.full(m_sc.shape, -jnp.inf, F32)
    acc_sc[...] = jnp.zeros(acc_sc.shape, F32)

    def scores(kb, s_sc):
        kblk = k_ref[pl.ds(pl.multiple_of(kb * t, t), t), :]
        for mp in range(2):
            s_sc[mp] = lax.dot_general(kblk[:, mp * dh:(mp + 1) * dh], q[:, mp * dh:(mp + 1) * dh],
                                       (((1,), (1,)), ((), ())), preferred_element_type=F32)

    def consume(kb, s_sc, masked):
        vt = vt_ref[kb]
        if masked:
            keys = lax.broadcasted_iota(jnp.int32, (t, t), 0)
            qs = lax.broadcasted_iota(jnp.int32, (t, t), 1)
            visible = (keys // chunk) <= (qs // chunk)
        for mp in range(2):
            st = s_sc[mp]
            if masked:
                st = jnp.where(visible, st, -jnp.inf)
            m_prev = m_sc[mp]
            m_new = jnp.maximum(m_prev, jnp.max(st, axis=0, keepdims=True))
            alpha = jnp.exp2(m_prev - m_new)
            p = jnp.exp2(st - m_new)
            acc_sc[mp] = alpha * acc_sc[mp] + jnp.dot(vt, p.astype(BF16), preferred_element_type=F32)
            m_sc[mp] = m_new

    nfull = qi
    scores(0, sa_sc)

    def pair(kb):
        consume(kb, sa_sc, False)
        scores(kb + 1, sb_sc)
        consume(kb + 1, sb_sc, False)
        scores(kb + 2, sa_sc)

    def quad_body(i, c):
        pair(4 * i)
        pair(4 * i + 2)
        return c

    def pair_body(i, c):
        pair(4 * (nfull // 4) + 2 * i)
        return c

    lax.fori_loop(0, nfull // 4, quad_body, 0)
    lax.fori_loop(0, (nfull % 4) // 2, pair_body, 0)

    @pl.when(nfull % 2 == 0)
    def _():
        consume(nfull, sa_sc, True)

    @pl.when(nfull % 2 == 1)
    def _():
        consume(nfull - 1, sa_sc, False)
        scores(nfull, sb_sc)
        consume(nfull, sb_sc, True)

    lam = _lambda(lamv_ref[...], lam_init)
    ot = (acc_sc[0, :dv, :] / acc_sc[0, dv:dv + 1, :]
          - lam * (acc_sc[1, :dv, :] / acc_sc[1, dv:dv + 1, :]))
    o_ref[...] = _attn_finish(ot.T, g_ref[...], sw_ref[...], lam_init, eps)


def _attn_prompt(cfg, q, k, vt3, g, lamv, sw, lam_init, name):
    L = q.shape[0]
    H, dv, dh = cfg.a_heads, cfg.a_vdim, cfg.a_head_dim
    tq = tk = vt3.shape[2]
    dva = vt3.shape[1] // H
    assert vt3.shape[0] * tk == L and tk % cfg.chunk == 0
    kern = functools.partial(_attn_prompt_kernel, t=tk, dh=dh, dv=dv, chunk=cfg.chunk,
                             lam_init=lam_init, eps=cfg.rms_eps)
    return pl.pallas_call(
        kern,
        grid=(H, L // tq),
        in_specs=[pl.BlockSpec((4, dh), lambda h, i: (0, 0)),
                  pl.BlockSpec((tq, dv), lambda h, i: (i, h)),
                  pl.BlockSpec((L, dv), lambda h, i: (0, h)),
                  pl.BlockSpec((L // tk, dva, tk), lambda h, i: (0, h, 0)),
                  pl.BlockSpec((tq, dv), lambda h, i: (i, h)),
                  pl.BlockSpec((1, dv), lambda h, i: (0, 0))],
        out_specs=pl.BlockSpec((tq, dv), lambda h, i: (i, h)),
        out_shape=jax.ShapeDtypeStruct((L, H * dv), BF16),
        scratch_shapes=[pltpu.VMEM((2, 1, tq), F32), pltpu.VMEM((2, dva, tq), F32),
                        pltpu.VMEM((2, tk, tq), F32), pltpu.VMEM((2, tk, tq), F32)],
        compiler_params=_cp(("parallel", "arbitrary")),
        name=name,
    )(lamv, q, k, vt3, g, sw)


def _attn_sample_kernel(lamv_ref, q_ref, kn_ref, vn_ref, kc_ref, vc_ref, g_ref, sw_ref, o_ref,
                        *, P, Ls, dh, hb, chunk, lam_init, eps):
    dv = 2 * dh
    R = hb * Ls
    C = P * hb

    def by_head(ref, lo, width):
        return jnp.concatenate([ref[:, hh * dv + lo:hh * dv + lo + width] for hh in range(hb)], axis=0)

    kc = kc_ref[0].reshape(C, dv).astype(BF16)
    vc = vc_ref[0].reshape(C, dv).astype(BF16)
    vn = by_head(vn_ref, 0, dv)
    row_c = lax.broadcasted_iota(jnp.int32, (R, C), 0)
    col_c = lax.broadcasted_iota(jnp.int32, (R, C), 1)
    vis_c = ((row_c // Ls == col_c % hb)
             & ((col_c // hb) // chunk <= (P + row_c % Ls) // chunk))
    row_n = lax.broadcasted_iota(jnp.int32, (R, R), 0)
    col_n = lax.broadcasted_iota(jnp.int32, (R, R), 1)
    vis_n = ((row_n // Ls == col_n // Ls)
             & ((P + col_n % Ls) // chunk <= (P + row_n % Ls) // chunk))
    dn = (((1,), (1,)), ((), ()))
    outs = []
    for mp in range(2):
        qm = by_head(q_ref, mp * dh, dh)
        knm = by_head(kn_ref, mp * dh, dh)
        sc = lax.dot_general(qm, kc[:, mp * dh:(mp + 1) * dh], dn, preferred_element_type=F32)
        sn = lax.dot_general(qm, knm, dn, preferred_element_type=F32)
        sc = jnp.where(vis_c, sc, -jnp.inf)
        sn = jnp.where(vis_n, sn, -jnp.inf)
        m = jnp.maximum(jnp.max(sc, axis=-1, keepdims=True), jnp.max(sn, axis=-1, keepdims=True))
        pc = jnp.exp2(sc - m)
        pn = jnp.exp2(sn - m)
        l = jnp.sum(pc, axis=-1, keepdims=True) + jnp.sum(pn, axis=-1, keepdims=True)
        acc = (jnp.dot(pc.astype(BF16), vc, preferred_element_type=F32)
               + jnp.dot(pn.astype(BF16), vn, preferred_element_type=F32))
        outs.append(acc / l)
    lam = _lambda(lamv_ref[...], lam_init)
    y = _attn_finish(outs[0] - lam * outs[1], by_head(g_ref, 0, dv), sw_ref[...], lam_init, eps)
    for hh in range(hb):
        o_ref[:, hh * dv:(hh + 1) * dv] = y[hh * Ls:(hh + 1) * Ls, :]


def _attn_sample(cfg, q, k, v, g, kc, vc, lamv, sw, lam_init, name):
    B, Ls, P = cfg.dec_batch, cfg.dec_seq, cfg.past_len
    H, dv, dh = cfg.a_heads, cfg.a_vdim, cfg.a_head_dim
    hb = _tile(H, SUBLANES, SUBLANES)
    kern = functools.partial(_attn_sample_kernel, P=P, Ls=Ls, dh=dh, hb=hb, chunk=cfg.chunk,
                             lam_init=lam_init, eps=cfg.rms_eps)
    row = pl.BlockSpec((Ls, hb * dv), lambda b, h: (b, h))
    cache = pl.BlockSpec((1, P, hb, dv), lambda b, h: (b, 0, h, 0))
    return pl.pallas_call(
        kern,
        grid=(B, H // hb),
        in_specs=[pl.BlockSpec((4, dh), lambda b, h: (0, 0)), row, row, row, cache, cache, row,
                  pl.BlockSpec((1, dv), lambda b, h: (0, 0))],
        out_specs=row,
        out_shape=jax.ShapeDtypeStruct((B * Ls, H * dv), BF16),
        compiler_params=_cp(("parallel", "parallel")),
        name=name,
    )(lamv, q, k, v, kc, vc, g, sw)


def _attn_layer(cfg, li, xp, xs, cache_k, cache_v, nw, w_in, lq1, lk1, lq2, lk2, subln, w_out, ropes):
    lam_init = 0.8 - 0.6 * math.exp(-0.3 * li)
    W = cfg.a_width
    w_in_b = w_in.astype(BF16)
    w_out_b = w_out.astype(BF16)
    lamv = jnp.stack([lq1, lk1, lq2, lk2]).astype(F32)
    sw = subln.reshape(1, cfg.a_vdim).astype(F32)
    scale = cfg.a_head_dim ** -0.5 * math.log2(math.e)
    res = []
    for tag, x, (cos, sin) in (("p", xp, ropes[0]), ("s", xs, ropes[1])):
        nm = f"l{li}{tag}"
        h = _rmsnorm(x, nw, BF16, cfg.rms_eps, nm + "_norm")
        tm, tn = cfg.mm_tm, cfg.mm_tn
        rope_ex = [(cos, "row"), (sin, "row")]
        (q,) = _matmul(h, w_in_b, 0, W, [BF16], functools.partial(_epi_rope_q, scale=scale), rope_ex,
                       tm, tn, nm + "_q")
        (g,) = _matmul(h, w_in_b, 3 * W, W, [BF16], _epi_plain, [], tm, tn, nm + "_g")
        if tag == "p":
            L = x.shape[0]
            H, dv, pad = cfg.a_heads, cfg.a_vdim, 16
            dva = dv + pad
            tk = _tile(L, cfg.attn_tk, LANES)
            hb = _tile(H, 8, 8)
            heads_out = (jax.ShapeDtypeStruct((L, H, dv), F32),
                         lambda bm, bn: ((bm, bn // dv, dv), lambda i, j: (i, j, 0)))
            vt_out = (jax.ShapeDtypeStruct((L // tk, H * dva, tk), BF16),
                      lambda bm, bn: ((bm // tk, bn // dv * dva, tk), lambda i, j: (i, j, 0)))
            k32, k16 = _matmul(h, w_in_b, W, W, [heads_out, BF16], functools.partial(_epi_rope_k_heads, dv=dv),
                               rope_ex, tk, hb * dv, nm + "_k", weights_outer=True)
            v32, vt3 = _matmul(h, w_in_b, 2 * W, W, [heads_out, vt_out],
                               functools.partial(_epi_v_transposed, tk=tk, dv=dv, pad=pad), [], tk, hb * dv,
                               nm + "_v", weights_outer=True)
            y = _attn_prompt(cfg, q, k16, vt3, g, lamv, sw, lam_init, nm + "_attn")
        else:
            k32, k16 = _matmul(h, w_in_b, W, W, [F32, BF16], _epi_rope_k, rope_ex, tm, tn, nm + "_k")
            v32, v16 = _matmul(h, w_in_b, 2 * W, W, [F32, BF16], _epi_two, [], tm, tn, nm + "_v")
            y = _attn_sample(cfg, q, k16, v16, g, cache_k, cache_v, lamv, sw, lam_init, nm + "_attn")
        (xn,) = _matmul(y, w_out_b, 0, cfg.d_model, [F32], _epi_residual, [(x, "tile")], tm, tn, nm + "_out")
        res.append((xn, k32, v32))
    return res


def _conv_kernel(x_ref, prev_ref, w_ref, b_ref, o_ref, carry_sc, *, tr, taps):
    r = pl.program_id(2)

    @pl.when(r == 0)
    def _():
        carry_sc[...] = prev_ref[0]

    x = x_ref[...].astype(F32)
    w = w_ref[...]
    b = b_ref[...]
    acc = b + x * w[taps - 1:taps, :]
    for j in range(1, taps):
        acc = acc + pltpu.roll(x, j, axis=0) * w[taps - 1 - j:taps - j, :]
    o_ref[...] = _silu(acc).astype(o_ref.dtype)
    x8 = x[0:SUBLANES, :]
    pr = carry_sc[...]
    row8 = lax.broadcasted_iota(jnp.int32, x8.shape, 0)
    acc8 = b + x8 * w[taps - 1:taps, :]
    for j in range(1, taps):
        sh = jnp.where(row8 < j, pltpu.roll(pr, j, axis=0), pltpu.roll(x8, j, axis=0))
        acc8 = acc8 + sh * w[taps - 1 - j:taps - j, :]
    o_ref[0:SUBLANES, :] = _silu(acc8).astype(o_ref.dtype)
    carry_sc[...] = x[tr - SUBLANES:tr, :]


def _conv_silu(xbc, prev8, w_t, b, nseq, name):
    R, C = xbc.shape
    taps = w_t.shape[0]
    Lseq = R // nseq
    tr = _tile(Lseq, 512, 16)
    tc = _tile(C, 512, LANES)
    nr = Lseq // tr
    return pl.pallas_call(
        functools.partial(_conv_kernel, tr=tr, taps=taps),
        grid=(C // tc, nseq, nr),
        in_specs=[pl.BlockSpec((tr, tc), lambda c, s, r: (s * nr + r, c)),
                  pl.BlockSpec((1, SUBLANES, tc), lambda c, s, r: (s, 0, c)),
                  pl.BlockSpec((taps, tc), lambda c, s, r: (0, c)),
                  pl.BlockSpec((1, tc), lambda c, s, r: (0, c))],
        out_specs=pl.BlockSpec((tr, tc), lambda c, s, r: (s * nr + r, c)),
        out_shape=jax.ShapeDtypeStruct((R, C), BF16),
        scratch_shapes=[pltpu.VMEM((SUBLANES, tc), F32)],
        compiler_params=_cp(("parallel", "arbitrary", "arbitrary")),
        name=name,
    )(xbc, prev8, w_t, b)


def _split_bf16(v):
    hi = v.astype(BF16)
    lo = (v - hi.astype(F32)).astype(BF16)
    return hi, lo


def _expand(v, e):
    hi, lo = _split_bf16(v)
    return (jnp.dot(hi, e, preferred_element_type=F32) + jnp.dot(lo, e, preferred_element_type=F32))


def _ssd_kernel(*refs, T, gw, hg, hd, N, has_init):
    if has_init:
        x_ref, b_ref, c_ref, dt_ref, alog_ref, d_ref, e_ref, init_ref, y_ref, st_ref, s_sc = refs
    else:
        x_ref, b_ref, c_ref, dt_ref, alog_ref, d_ref, e_ref, y_ref, st_ref, s_sc = refs
    ci = pl.program_id(2)

    @pl.when(ci == 0)
    def _():
        if has_init:
            s_sc[...] = init_ref[0].reshape(gw, N).T
        else:
            s_sc[...] = jnp.zeros(s_sc.shape, F32)

    dt = dt_ref[...]
    a_neg = -jnp.exp(alog_ref[...])
    a = dt * a_neg
    row = lax.broadcasted_iota(jnp.int32, a.shape, 0)
    d = 1
    while d < T:
        a = a + jnp.where(row >= d, pltpu.roll(a, d, axis=0), 0.0)
        d *= 2
    a_last = a[T - 1:T, :]
    e = e_ref[...]
    expa = jnp.exp(a)
    expa_x = jnp.dot(expa.astype(BF16), e, preferred_element_type=F32)
    dtdecay_x = jnp.dot((dt * jnp.exp(a_last - a)).astype(BF16), e, preferred_element_type=F32)
    exp_last_x = _expand(expa[T - SUBLANES:T, :], e)[SUBLANES - 1:SUBLANES, :]
    xb = x_ref[...]
    x = xb.astype(F32)
    bm = b_ref[...]
    cm = c_ref[...]
    cb = lax.dot_general(cm, bm, (((1,), (1,)), ((), ())), preferred_element_type=F32)
    s0 = s_sc[...]
    y_off = jnp.dot(cm, s0.astype(BF16), preferred_element_type=F32) * expa_x
    a2_t = (a - jnp.log(dt)).T
    tri = (lax.broadcasted_iota(jnp.int32, (T, T), 0) >= lax.broadcasted_iota(jnp.int32, (T, T), 1))
    lane = lax.broadcasted_iota(jnp.int32, (T, LANES), 1)
    per_pair = LANES // hd
    pieces = []
    for pi in range(gw // LANES):
        xp = xb[:, pi * LANES:(pi + 1) * LANES]
        piece = None
        for e_i in range(per_pair):
            hl = pi * per_pair + e_i
            seg = a[:, hl:hl + 1] - a2_t[hl:hl + 1, :]
            m = (cb * jnp.where(tri, jnp.exp(seg), 0.0)).astype(BF16)
            r = jnp.dot(m, xp, preferred_element_type=F32)
            if piece is None:
                piece = r
            else:
                piece = jnp.where(lane < e_i * hd, piece, r)
        pieces.append(piece)
    y_diag = pieces[0] if len(pieces) == 1 else jnp.concatenate(pieces, axis=1)
    y_ref[...] = (y_diag + y_off + d_ref[...] * x).astype(y_ref.dtype)
    xw = (x * dtdecay_x).astype(BF16)
    upd = lax.dot_general(bm, xw, (((0,), (0,)), ((), ())), preferred_element_type=F32)
    s_new = s0 * exp_last_x + upd
    s_sc[...] = s_new

    @pl.when(ci == pl.num_programs(2) - 1)
    def _():
        st_ref[0] = s_new.T.reshape(hg, hd, N)


def _ssd(cfg, xbc_act, dtg, alog_g, d_exp, e_mat, init, nseq, T, name):
    R = xbc_act.shape[0]
    G, N, hd = cfg.b_groups, cfg.b_state, cfg.b_head_dim
    hg = cfg.b_heads // G
    gw = hg * hd
    Lseq = R // nseq
    nc = Lseq // T
    bw = cfg.b_width
    boff = bw // N
    has_init = init is not None
    kern = functools.partial(_ssd_kernel, T=T, gw=gw, hg=hg, hd=hd, N=N, has_init=has_init)
    in_specs = [pl.BlockSpec((T, gw), lambda s, g, c: (s * nc + c, g)),
                pl.BlockSpec((T, N), lambda s, g, c: (s * nc + c, boff + g)),
                pl.BlockSpec((T, N), lambda s, g, c: (s * nc + c, boff + G + g)),
                pl.BlockSpec((T, LANES), lambda s, g, c: (s * nc + c, g)),
                pl.BlockSpec((1, LANES), lambda s, g, c: (0, g)),
                pl.BlockSpec((1, gw), lambda s, g, c: (0, g)),
                pl.BlockSpec((LANES, gw), lambda s, g, c: (0, 0))]
    args = [xbc_act, xbc_act, xbc_act, dtg, alog_g, d_exp, e_mat]
    if has_init:
        in_specs.append(pl.BlockSpec((1, hg, hd, N), lambda s, g, c: (s, g, 0, 0)))
        args.append(init)
    return pl.pallas_call(
        kern,
        grid=(nseq, G, nc),
        in_specs=in_specs,
        out_specs=[pl.BlockSpec((T, gw), lambda s, g, c: (s * nc + c, g)),
                   pl.BlockSpec((1, hg, hd, N), lambda s, g, c: (s, g, 0, 0))],
        out_shape=[jax.ShapeDtypeStruct((R, bw), BF16),
                   jax.ShapeDtypeStruct((nseq, cfg.b_heads, hd, N), F32)],
        scratch_shapes=[pltpu.VMEM((N, gw), F32)],
        compiler_params=_cp(("parallel", "parallel", "arbitrary")),
        name=name,
    )(*args)


def _ssd_layer(cfg, li, xp, xs, conv_prev, ssm_prev, nw, w_in, conv_w, conv_b, dt_bias, a_log, d_skip,
               gnorm, w_out):
    bw, cd, H, G = cfg.b_width, cfg.b_conv_dim, cfg.b_heads, cfg.b_groups
    hg = H // G
    hd = cfg.b_head_dim
    gw = hg * hd
    w_in_b = w_in.astype(BF16)
    w_out_b = w_out.astype(BF16)

    def per_group(v):
        lead = v.shape[:-1]
        v = v.reshape(lead + (G, hg))
        v = jnp.pad(v, [(0, 0)] * len(lead) + [(0, 0), (0, LANES - hg)])
        return v.reshape(lead + (G * LANES,))

    w_dt_g = per_group(w_in[:, bw + cd:]).astype(BF16)
    dtb_g = per_group(dt_bias.astype(F32)).reshape(1, G * LANES)
    alog_g = per_group(a_log.astype(F32)).reshape(1, G * LANES)
    d_exp = jnp.repeat(d_skip.astype(F32), hd).reshape(1, bw)
    e_mat = (jnp.arange(LANES)[:, None] == (jnp.arange(gw)[None, :] // hd)).astype(BF16)
    w_t = conv_w.astype(F32).T
    cb2 = conv_b.astype(F32).reshape(1, cd)
    T = cfg.ssd_chunk
    B, Ls = cfg.dec_batch, cfg.dec_seq
    res = []
    for tag, x in (("p", xp), ("s", xs)):
        nm = f"l{li}{tag}"
        tm, tn = cfg.mm_tm, cfg.mm_tn
        h = _rmsnorm(x, nw, BF16, cfg.rms_eps, nm + "_norm")
        (z,) = _matmul(h, w_in_b, 0, bw, [BF16], _epi_plain, [], tm, tn, nm + "_z")
        (xbc,) = _matmul(h, w_in_b, bw, cd, [BF16], _epi_plain, [], tm, tn, nm + "_xbc")
        (dtg,) = _matmul(h, w_dt_g, 0, G * LANES, [F32], _epi_softplus_bias, [(dtb_g, "col")], tm, tn,
                         nm + "_dt")
        if tag == "p":
            prev8 = jnp.zeros((1, SUBLANES, cd), F32)
            act = _conv_silu(xbc, prev8, w_t, cb2, 1, nm + "_conv")
            conv_new = xbc[x.shape[0] - (cfg.b_conv - 1):].astype(F32)[None]
            y, st = _ssd(cfg, act, dtg, alog_g, d_exp, e_mat, None, 1, T, nm + "_ssd")
        else:
            prev8 = jnp.pad(conv_prev.astype(F32), ((0, 0), (SUBLANES - (cfg.b_conv - 1), 0), (0, 0)))
            act = _conv_silu(xbc, prev8, w_t, cb2, B, nm + "_conv")
            conv_new = xbc.reshape(B, Ls, cd)[:, Ls - (cfg.b_conv - 1):].astype(F32)
            act_p = jnp.pad(act.reshape(B, Ls, cd), ((0, 0), (0, T - Ls), (0, 0))).reshape(B * T, cd)
            dt_p = jnp.pad(dtg.reshape(B, Ls, G * LANES), ((0, 0), (0, T - Ls), (0, 0))).reshape(B * T, G * LANES)
            y_p, st = _ssd(cfg, act_p, dt_p, alog_g, d_exp, e_mat, ssm_prev.astype(F32), B, T, nm + "_ssd")
            y = y_p.reshape(B, T, bw)[:, :Ls].reshape(B * Ls, bw)
        yn = _gated_rmsnorm(y, z, gnorm, cfg.rms_eps, nm + "_gnorm")
        (xn,) = _matmul(yn, w_out_b, 0, cfg.d_model, [F32], _epi_residual, [(x, "tile")], tm // 2, tn,
                        nm + "_out")
        res.append((xn, conv_new, st))
    return res


def _s5_disc_kernel(lr_ref, li_ref, ls_ref, br_ref, bi_ref, ar_ref, ai_ref, bbr_ref, bbi_ref):
    lr = lr_ref[...]
    li = li_ref[...]
    step = jnp.exp(ls_ref[...])
    mag = jnp.exp(lr * step)
    ar = mag * jnp.cos(li * step)
    ai = mag * jnp.sin(li * step)
    den = lr * lr + li * li
    nr = ar - 1.0
    ni = ai
    cr = ((nr * lr + ni * li) / den)[:, None, :]
    ci = ((ni * lr - nr * li) / den)[:, None, :]
    br = br_ref[...]
    bi = bi_ref[...]
    ar_ref[...] = ar
    ai_ref[...] = ai
    bbr_ref[...] = cr * br - ci * bi
    bbi_ref[...] = cr * bi + ci * br


def _s5_discretize(cfg, lam_re, lam_im, log_step, b_re, b_im, name):
    G, P, K = cfg.c_groups, cfg.c_state, cfg.c_group
    bt_r = jnp.transpose(b_re.astype(F32), (0, 2, 1))
    bt_i = jnp.transpose(b_im.astype(F32), (0, 2, 1))
    tg = _tile(G, 64, 8)
    s2 = pl.BlockSpec((tg, P), lambda i: (i, 0))
    s3 = pl.BlockSpec((tg, K, P), lambda i: (i, 0, 0))
    return pl.pallas_call(
        _s5_disc_kernel,
        grid=(G // tg,),
        in_specs=[s2, s2, pl.BlockSpec((tg, 1), lambda i: (i, 0)), s3, s3],
        out_specs=[s2, s2, s3, s3],
        out_shape=[jax.ShapeDtypeStruct((G, P), F32)] * 2 + [jax.ShapeDtypeStruct((G, K, P), F32)] * 2,
        compiler_params=_cp(("parallel",)),
        name=name,
    )(lam_re.astype(F32), lam_im.astype(F32), log_step.astype(F32).reshape(G, 1), bt_r, bt_i)


def _gelu_tanh(v):
    return 0.5 * v * (1.0 + jnp.tanh(math.sqrt(2.0 / math.pi) * (v + 0.044715 * (v * v * v))))


def _s5_scan_kernel(*refs, tT, pitch, nsub, sw, emit_y):
    if emit_y:
        (u_ref, bbd_ref, cbd_ref, ar_ref, ai_ref, d_ref, hr0_ref, hi0_ref,
         yg_ref, hrT_ref, hiT_ref, bu_a, bu_b, hs_a, hs_b, hr_sc, hi_sc) = refs
    else:
        (u_ref, bbd_ref, ar_ref, ai_ref, hr0_ref, hi0_ref,
         hrT_ref, hiT_ref, bu_a, bu_b, hr_sc, hi_sc) = refs
        hs_a = hs_b = None
    tb = pl.program_id(2)

    @pl.when(tb == 0)
    def _():
        hr_sc[...] = hr0_ref[...]
        hi_sc[...] = hi0_ref[...]

    nl = sw // LANES
    ar = jnp.broadcast_to(ar_ref[...], (SUBLANES, sw))
    ai = jnp.broadcast_to(ai_ref[...], (SUBLANES, sw))
    hr = hr_sc[...]
    hi = hi_sc[...]
    ts = tT // nsub
    bufs = ((bu_a, hs_a), (bu_b, hs_b))

    def in_map(sb):
        bu_sc = bufs[sb % 2][0]
        ub = u_ref[:, sb * ts:(sb + 1) * ts, :].reshape(SUBLANES * ts, LANES)
        bu = jnp.dot(ub, bbd_ref[0], preferred_element_type=F32)
        for l in range(2 * nl):
            for sq in range(SUBLANES):
                bu_sc[l, sq * pitch:sq * pitch + ts, :] = bu[sq * ts:(sq + 1) * ts, l * LANES:(l + 1) * LANES]

    def out_map(sb):
        hs_sc = bufs[sb % 2][1]
        ub = u_ref[:, sb * ts:(sb + 1) * ts, :].reshape(SUBLANES * ts, LANES)
        hs = jnp.concatenate(
            [jnp.concatenate([hs_sc[l, sq * pitch:sq * pitch + ts, :] for sq in range(SUBLANES)], axis=0)
             for l in range(2 * nl)], axis=1).astype(BF16)
        y = jnp.dot(hs, cbd_ref[0], preferred_element_type=F32)
        v = y + d_ref[...] * ub.astype(F32)
        yg_ref[:, sb * ts:(sb + 1) * ts, :] = _gelu_tanh(v).reshape(SUBLANES, ts, LANES).astype(yg_ref.dtype)

    in_map(0)
    for sb in range(nsub):
        bu_sc, hs_sc = bufs[sb % 2]
        if sb + 1 < nsub:
            in_map(sb + 1)
        if emit_y and sb >= 1:
            out_map(sb - 1)
        for t in range(ts):
            rows = pl.ds(t, SUBLANES, stride=pitch)
            bur = jnp.concatenate([bu_sc[l, rows, :] for l in range(nl)], axis=1)
            bui = jnp.concatenate([bu_sc[nl + l, rows, :] for l in range(nl)], axis=1)
            hr, hi = ar * hr - ai * hi + bur, ar * hi + ai * hr + bui
            if emit_y:
                for l in range(nl):
                    hs_sc[l, rows, :] = hr[:, l * LANES:(l + 1) * LANES]
                    hs_sc[nl + l, rows, :] = hi[:, l * LANES:(l + 1) * LANES]
    if emit_y:
        out_map(nsub - 1)
    hr_sc[...] = hr
    hi_sc[...] = hi

    @pl.when(tb == pl.num_programs(2) - 1)
    def _():
        hrT_ref[...] = hr
        hiT_ref[...] = hi


def _s5_scan(cfg, u3, bbd, cbd, ar, ai, d2, hr0, hi0, emit_y, name):
    NS, T, C = u3.shape
    nj = C // LANES
    sw = bbd.shape[2] // 2
    tT = _tile(T, cfg.s5_tt, 16)
    nsg = NS // SUBLANES
    nsub = max(1, tT // 16)
    ts = tT // nsub
    pitch = ts + SUBLANES if (ts // SUBLANES) % 2 == 0 else ts
    kern = functools.partial(_s5_scan_kernel, tT=tT, pitch=pitch, nsub=nsub, sw=sw, emit_y=emit_y)
    u_spec = pl.BlockSpec((SUBLANES, tT, LANES), lambda j, s, t: (s, t, j))
    bbd_spec = pl.BlockSpec((1, LANES, 2 * sw), lambda j, s, t: (j, 0, 0))
    a_spec = pl.BlockSpec((1, sw), lambda j, s, t: (0, j))
    st_spec = pl.BlockSpec((SUBLANES, sw), lambda j, s, t: (s, j))
    st_shape = jax.ShapeDtypeStruct((NS, nj * sw), F32)
    slab = pltpu.VMEM((2 * sw // LANES, SUBLANES * pitch, LANES), F32)
    scratch = [slab, slab]
    if emit_y:
        in_specs = [u_spec, bbd_spec, pl.BlockSpec((1, 2 * sw, LANES), lambda j, s, t: (j, 0, 0)),
                    a_spec, a_spec, pl.BlockSpec((1, LANES), lambda j, s, t: (0, j)), st_spec, st_spec]
        args = [u3, bbd, cbd, ar, ai, d2, hr0, hi0]
        out_specs = [u_spec, st_spec, st_spec]
        out_shape = [jax.ShapeDtypeStruct((NS, T, C), BF16), st_shape, st_shape]
        scratch += [slab, slab]
    else:
        in_specs = [u_spec, bbd_spec, a_spec, a_spec, st_spec, st_spec]
        args = [u3, bbd, ar, ai, hr0, hi0]
        out_specs = [st_spec, st_spec]
        out_shape = [st_shape, st_shape]
    scratch += [pltpu.VMEM((SUBLANES, sw), F32)] * 2
    return pl.pallas_call(
        kern,
        grid=(nj, nsg, T // tT),
        in_specs=in_specs, out_specs=out_specs, out_shape=out_shape,
        scratch_shapes=scratch,
        compiler_params=_cp(("parallel", "parallel", "arbitrary")),
        name=name,
    )(*args)


def _s5_combine_kernel(er_ref, ei_ref, ar_ref, ai_ref, h0r_ref, h0i_ref, hr_ref, hi_ref, *, nseg, nsq):
    pr = ar_ref[...]
    pi_ = ai_ref[...]
    for _ in range(nsq):
        pr, pi_ = pr * pr - pi_ * pi_, 2.0 * pr * pi_
    hr = h0r_ref[...]
    hi = h0i_ref[...]
    hr_ref[0:1, :] = hr
    hi_ref[0:1, :] = hi
    for i in range(1, nseg):
        er = er_ref[i - 1:i, :]
        ei = ei_ref[i - 1:i, :]
        hr, hi = pr * hr - pi_ * hi + er, pr * hi + pi_ * hr + ei
        hr_ref[i:i + 1, :] = hr
        hi_ref[i:i + 1, :] = hi


def _s5_combine(er, ei, ar, ai, h0r, h0i, seg_len, name):
    nseg, W = er.shape
    nsq = int(round(math.log2(seg_len)))
    assert 2 ** nsq == seg_len
    tw = _tile(W, 1024, LANES)
    full = pl.BlockSpec((nseg, tw), lambda i: (0, i))
    row = pl.BlockSpec((1, tw), lambda i: (0, i))
    return pl.pallas_call(
        functools.partial(_s5_combine_kernel, nseg=nseg, nsq=nsq),
        grid=(W // tw,),
        in_specs=[full, full, row, row, row, row],
        out_specs=[full, full],
        out_shape=[jax.ShapeDtypeStruct((nseg, W), F32)] * 2,
        compiler_params=_cp(("parallel",)),
        name=name,
    )(er, ei, ar, ai, h0r, h0i)


def _s5_layer(cfg, li, xp, xs, s_re, s_im, nw, w_in, lam_re, lam_im, log_step, b_re, b_im, c_re, c_im,
              d_skip, w_glu, b_glu, w_out):
    CW, G, P, K = cfg.c_width, cfg.c_groups, cfg.c_state, cfg.c_group
    gpb = LANES // K
    nj = CW // LANES
    sw = gpb * P
    w_in_b = w_in.astype(BF16)
    w_glu_b = w_glu.astype(BF16)
    w_out_b = w_out.astype(BF16)
    ar, ai, bbr, bbi = _s5_discretize(cfg, lam_re, lam_im, log_step, b_re, b_im, f"l{li}_disc")
    eye = jnp.eye(gpb, dtype=F32)
    bb = jnp.stack([bbr, bbi]).reshape(2, nj, gpb, K, P)
    bbd = jnp.einsum("rjgkp,gh->jgkrhp", bb, eye).reshape(nj, LANES, 2 * sw).astype(BF16)
    cc = jnp.stack([c_re.astype(F32), -c_im.astype(F32)]).reshape(2, nj, gpb, K, P)
    cbd = jnp.einsum("rjgkp,gh->jrgphk", cc, eye).reshape(nj, 2 * sw, LANES).astype(BF16)
    ar2 = ar.reshape(1, G * P)
    ai2 = ai.reshape(1, G * P)
    d2 = d_skip.astype(F32).reshape(1, CW)
    bg2 = b_glu.astype(F32).reshape(1, CW)
    B, Ls = cfg.dec_batch, cfg.dec_seq
    res = []
    for tag, x in (("p", xp), ("s", xs)):
        nm = f"l{li}{tag}"
        tm, tn = cfg.mm_tm, cfg.mm_tn
        h = _rmsnorm(x, nw, BF16, cfg.rms_eps, nm + "_norm")
        (u,) = _matmul(h, w_in_b, 0, CW, [BF16], _epi_plain, [], tm, tn, nm + "_u")
        (g,) = _matmul(h, w_in_b, CW, CW, [BF16], _epi_plain, [], tm, tn, nm + "_g")
        if tag == "p":
            L = x.shape[0]
            seg = _tile(L // SUBLANES, cfg.s5_seg, 16)
            nseg = L // seg
            u3 = u.reshape(nseg, seg, CW)
            zs = jnp.zeros((nseg, G * P), F32)
            er, ei = _s5_scan(cfg, u3, bbd, None, ar2, ai2, None, zs, zs, False, nm + "_scan_a")
            z1 = jnp.zeros((1, G * P), F32)
            h0r, h0i = _s5_combine(er, ei, ar2, ai2, z1, z1, seg, nm + "_comb")
            yg3, hrT, hiT = _s5_scan(cfg, u3, bbd, cbd, ar2, ai2, d2, h0r, h0i, True, nm + "_scan_b")
            yg = yg3.reshape(L, CW)
            st_r = hrT[nseg - 1:nseg].reshape(1, G, P)
            st_i = hiT[nseg - 1:nseg].reshape(1, G, P)
        else:
            u3 = u.reshape(B, Ls, CW)
            yg3, hrT, hiT = _s5_scan(cfg, u3, bbd, cbd, ar2, ai2, d2,
                                     s_re.astype(F32).reshape(B, G * P), s_im.astype(F32).reshape(B, G * P),
                                     True, nm + "_scan")
            yg = yg3.reshape(B * Ls, CW)
            st_r = hrT.reshape(B, G, P)
            st_i = hiT.reshape(B, G, P)
        (y2,) = _matmul(yg, w_glu_b, 0, CW, [BF16], _epi_glu, [(bg2, "col"), (yg, "tile"), (g, "tile")],
                        tm, tn, nm + "_glu")
        (xn,) = _matmul(y2, w_out_b, 0, cfg.d_model, [F32], _epi_residual, [(x, "tile")], tm, tn, nm + "_out")
        res.append((xn, st_r, st_i))
    return res


def _forward(cfg, x_prompt, x_sample,
             cache_k_l0, cache_v_l0, state_conv_l1, state_ssm_l1,
             state_s5_re_l2, state_s5_im_l2, cache_k_l3, cache_v_l3,
             norm_l0, attn_w_in_l0, lam_q1_l0, lam_k1_l0, lam_q2_l0, lam_k2_l0, subln_l0, attn_w_out_l0,
             norm_l1, ssm_w_in_l1, conv_w_l1, conv_b_l1, dt_bias_l1, a_log_l1, d_l1, gnorm_l1, ssm_w_out_l1,
             norm_l2, s5_w_in_l2, lam_re_l2, lam_im_l2, log_step_l2, b_re_l2, b_im_l2, c_re_l2, c_im_l2,
             d_l2, glu_w_l2, glu_b_l2, s5_w_out_l2,
             norm_l3, attn_w_in_l3, lam_q1_l3, lam_k1_l3, lam_q2_l3, lam_k2_l3, subln_l3, attn_w_out_l3,
             final_norm):
    D = cfg.d_model
    L, B, Ls, P = cfg.seq, cfg.dec_batch, cfg.dec_seq, cfg.past_len
    H, dv = cfg.a_heads, cfg.a_vdim
    xp = x_prompt.reshape(L, D).astype(F32)
    xs = x_sample.reshape(B * Ls, D).astype(F32)
    ropes = (_rope_tables(cfg, L, 0, L, "rope_p"), _rope_tables(cfg, B * Ls, P, Ls, "rope_s"))

    (xp, kp0, vp0), (xs, ks0, vs0) = _attn_layer(
        cfg, 0, xp, xs, cache_k_l0, cache_v_l0, norm_l0, attn_w_in_l0, lam_q1_l0, lam_k1_l0, lam_q2_l0,
        lam_k2_l0, subln_l0, attn_w_out_l0, ropes)
    (xp, conv_p1, ssm_p1), (xs, conv_s1, ssm_s1) = _ssd_layer(
        cfg, 1, xp, xs, state_conv_l1, state_ssm_l1, norm_l1, ssm_w_in_l1, conv_w_l1, conv_b_l1, dt_bias_l1,
        a_log_l1, d_l1, gnorm_l1, ssm_w_out_l1)
    (xp, s5re_p2, s5im_p2), (xs, s5re_s2, s5im_s2) = _s5_layer(
        cfg, 2, xp, xs, state_s5_re_l2, state_s5_im_l2, norm_l2, s5_w_in_l2, lam_re_l2, lam_im_l2,
        log_step_l2, b_re_l2, b_im_l2, c_re_l2, c_im_l2, d_l2, glu_w_l2, glu_b_l2, s5_w_out_l2)
    (xp, kp3, vp3), (xs, ks3, vs3) = _attn_layer(
        cfg, 3, xp, xs, cache_k_l3, cache_v_l3, norm_l3, attn_w_in_l3, lam_q1_l3, lam_k1_l3, lam_q2_l3,
        lam_k2_l3, subln_l3, attn_w_out_l3, ropes)

    y_prompt = _rmsnorm(xp, final_norm, F32, cfg.rms_eps, "final_p").reshape(1, L, D)
    y_sample = _rmsnorm(xs, final_norm, F32, cfg.rms_eps, "final_s").reshape(B, Ls, D)

    def kv_p(a):
        return a.reshape(1, L, H, dv)

    def kv_s(a):
        return a.reshape(B, Ls, H, dv)

    return (y_prompt, y_sample, kv_p(kp0), kv_p(vp0), kv_s(ks0), kv_s(vs0),
            conv_p1, ssm_p1, conv_s1, ssm_s1, s5re_p2, s5im_p2, s5re_s2, s5im_s2,
            kv_p(kp3), kv_p(vp3), kv_s(ks3), kv_s(vs3))


_CFG = Cfg()


def kernel(x_prompt, x_sample, cache_k_l0, cache_v_l0, state_conv_l1, state_ssm_l1, state_s5_re_l2, state_s5_im_l2, cache_k_l3, cache_v_l3, norm_l0, attn_w_in_l0, lam_q1_l0, lam_k1_l0, lam_q2_l0, lam_k2_l0, subln_l0, attn_w_out_l0, norm_l1, ssm_w_in_l1, conv_w_l1, conv_b_l1, dt_bias_l1, a_log_l1, d_l1, gnorm_l1, ssm_w_out_l1, norm_l2, s5_w_in_l2, lam_re_l2, lam_im_l2, log_step_l2, b_re_l2, b_im_l2, c_re_l2, c_im_l2, d_l2, glu_w_l2, glu_b_l2, s5_w_out_l2, norm_l3, attn_w_in_l3, lam_q1_l3, lam_k1_l3, lam_q2_l3, lam_k2_l3, subln_l3, attn_w_out_l3, final_norm):
    return _forward(_CFG, x_prompt, x_sample, cache_k_l0, cache_v_l0, state_conv_l1, state_ssm_l1, state_s5_re_l2, state_s5_im_l2, cache_k_l3, cache_v_l3, norm_l0, attn_w_in_l0, lam_q1_l0, lam_k1_l0, lam_q2_l0, lam_k2_l0, subln_l0, attn_w_out_l0, norm_l1, ssm_w_in_l1, conv_w_l1, conv_b_l1, dt_bias_l1, a_log_l1, d_l1, gnorm_l1, ssm_w_out_l1, norm_l2, s5_w_in_l2, lam_re_l2, lam_im_l2, log_step_l2, b_re_l2, b_im_l2, c_re_l2, c_im_l2, d_l2, glu_w_l2, glu_b_l2, s5_w_out_l2, norm_l3, attn_w_in_l3, lam_q1_l3, lam_k1_l3, lam_q2_l3, lam_k2_l3, subln_l3, attn_w_out_l3, final_norm)
```
